```python
import math
import jax, jax.numpy as jnp
from jax import lax
import numpy as np

D_MODEL = 1024
BATCH = 16
SEQ = 4096
DEPTH = 4

CHUNK = 64
N_A_LAYERS = max(1, DEPTH // 2)
N_B_LAYERS = DEPTH - N_A_LAYERS
N_HEADS = 16
HEAD_DIM = D_MODEL // N_HEADS
Q_BLOCK = 128
CONV_K = 31
N_EXPERTS = 16
N_GROUPS = 4
EXPERTS_PER_GROUP = N_EXPERTS // N_GROUPS
TOP_K = 2
D_EXPERT = D_MODEL
MOE_BLOCK = 256
ALPHA = (2.0 * DEPTH) ** 0.25
BETA = (8.0 * DEPTH) ** -0.25
LN_EPS = 1e-5

kernel_name = "yoco_conformer_fox_grouped_moe_deepnorm"


def layer_norm(x, g, b):
    xf = x.astype(jnp.float32)
    mu = xf.mean(-1, keepdims=True)
    var = jnp.square(xf - mu).mean(-1, keepdims=True)
    return ((xf - mu) * lax.rsqrt(var + LN_EPS)).astype(x.dtype) * g + b


def conformer_conv(h, w_pw1, b_pw1, w_dw, b_dw, ln_g, ln_b, w_pw2, b_pw2):
    D = h.shape[-1]
    u = h @ w_pw1 + b_pw1
    u = u[..., :D] * jax.nn.sigmoid(u[..., D:])
    y = lax.conv_general_dilated(
        u, w_dw, window_strides=(1,), padding=[(CONV_K - 1, 0)],
        dimension_numbers=('NWC', 'WIO', 'NWC'), feature_group_count=D) + b_dw
    y = jax.nn.silu(layer_norm(y, ln_g, ln_b))
    return y @ w_pw2 + b_pw2


def shared_kv(x, c_act, w_ada_kv, b_ada_kv, w_kvf, b_f):
    B, S, D = x.shape
    shift, scale = jnp.split((c_act @ w_ada_kv + b_ada_kv)[:, None, :], 2, axis=-1)
    hk = x * (1 + scale) + shift
    kvf = hk @ w_kvf
    k = kvf[..., :D].reshape(B, S, N_HEADS, HEAD_DIM)
    v = kvf[..., D:2 * D].reshape(B, S, N_HEADS, HEAD_DIM)
    log_f = jax.nn.log_sigmoid((kvf[..., 2 * D:] + b_f).astype(jnp.float32))
    fcum = jnp.cumsum(log_f, axis=1).transpose(0, 2, 1)
    return k, v, fcum


def fox_attention(q, k, v, fcum):
    S = q.shape[1]
    scale = HEAD_DIM ** -0.5
    outs = []
    for i in range(S // Q_BLOCK):
        q0, q1 = i * Q_BLOCK, (i + 1) * Q_BLOCK
        s = jnp.einsum('bqhd,bkhd->bhqk', q[:, q0:q1], k[:, :q1]).astype(jnp.float32) * scale
        s = s + fcum[:, :, q0:q1, None] - fcum[:, :, None, :q1]
        causal = (q0 + jnp.arange(Q_BLOCK))[:, None] >= jnp.arange(q1)[None, :]
        s = jnp.where(causal, s, -jnp.inf)
        p = jax.nn.softmax(s, axis=-1).astype(v.dtype)
        outs.append(jnp.einsum('bhqk,bkhd->bqhd', p, v[:, :q1]))
    return jnp.concatenate(outs, axis=1)


def grouped_moe(h, router_w, router_b, w_gate, w_up, w_down):
    B, S, D = h.shape
    N = B * S
    t = h.reshape(N, D)
    logits = (t @ router_w).astype(jnp.float32) + router_b.astype(jnp.float32)
    probs = jax.nn.softmax(logits, axis=-1).reshape(N, N_GROUPS, EXPERTS_PER_GROUP)
    group_score = lax.top_k(probs, TOP_K)[0].sum(-1)
    g_sel = jnp.argmax(group_score, axis=-1)
    in_group = jnp.take_along_axis(probs, g_sel[:, None, None], axis=1)[:, 0]
    w_top, i_top = lax.top_k(in_group, TOP_K)
    w_top = w_top / w_top.sum(-1, keepdims=True)
    expert_idx = (g_sel[:, None] * EXPERTS_PER_GROUP + i_top).reshape(-1)
    tok_idx = jnp.repeat(jnp.arange(N, dtype=jnp.int32), TOP_K)
    gate_w = w_top.reshape(-1)
    A = N * TOP_K
    order = jnp.argsort(expert_idx)
    e_sorted = expert_idx[order]
    counts = jnp.bincount(expert_idx, length=N_EXPERTS)
    padded = (counts + MOE_BLOCK - 1) // MOE_BLOCK * MOE_BLOCK
    pad_end = jnp.cumsum(padded)
    pad_start = pad_end - padded
    start = jnp.cumsum(counts) - counts
    dest = pad_start[e_sorted] + jnp.arange(A) - start[e_sorted]
    n_blocks = -(-A // MOE_BLOCK) + N_EXPERTS
    P = n_blocks * MOE_BLOCK
    slot_tok = jnp.full((P,), N, jnp.int32).at[dest].set(tok_idx[order])
    slot_w = jnp.zeros((P,), jnp.float32).at[dest].set(gate_w[order])
    block_expert = jnp.minimum(
        jnp.searchsorted(pad_end, jnp.arange(n_blocks) * MOE_BLOCK, side='right'), N_EXPERTS - 1)
    t_pad = jnp.concatenate([t, jnp.zeros((1, D), t.dtype)], axis=0)
    xs = t_pad[slot_tok].reshape(n_blocks, MOE_BLOCK, D)

    def expert_block(args):
        xb, e = args
        return (jax.nn.silu(xb @ w_gate[e]) * (xb @ w_up[e])) @ w_down[e]

    ys = lax.map(expert_block, (xs, block_expert)).reshape(P, D)
    ys = ys * slot_w[:, None].astype(ys.dtype)
    out = jnp.zeros((N + 1, D), ys.dtype).at[slot_tok].add(ys)[:N]
    return out.reshape(B, S, D)


def setup_inputs(seed: int = 0) -> dict:
    key = jax.random.key(seed)
    ks = jax.random.split(key, 32)
    D, H, F, E = D_MODEL, N_HEADS, D_EXPERT, N_EXPERTS
    f32 = jnp.float32

    def nrm(i, shape, scale):
        return jax.random.normal(ks[i], shape, f32) * scale

    w_kvf = jnp.concatenate([
        nrm(20, (D, D), D ** -0.5),
        nrm(21, (D, D), BETA * D ** -0.5),
        nrm(22, (D, H), 0.1 * D ** -0.5)], axis=1)
    return {
        "x": nrm(0, (BATCH, SEQ, D), 1.0),
        "c": nrm(1, (BATCH, D), 1.0),
        "w_ada": nrm(2, (DEPTH, D, 6 * D), 0.1 * D ** -0.5),
        "b_ada": nrm(3, (DEPTH, 6 * D), 0.01),
        "ln_mix_g": 1.0 + nrm(4, (DEPTH, D), 0.02),
        "ln_mix_b": nrm(5, (DEPTH, D), 0.02),
        "ln_ffn_g": 1.0 + nrm(6, (DEPTH, D), 0.02),
        "ln_ffn_b": nrm(7, (DEPTH, D), 0.02),
        "w_pw1": nrm(8, (N_A_LAYERS, D, 2 * D), D ** -0.5),
        "b_pw1": nrm(9, (N_A_LAYERS, 2 * D), 0.01),
        "w_dw": nrm(10, (N_A_LAYERS, CONV_K, 1, D), CONV_K ** -0.5),
        "b_dw": nrm(11, (N_A_LAYERS, D), 0.01),
        "conv_ln_g": 1.0 + nrm(12, (N_A_LAYERS, D), 0.02),
        "conv_ln_b": nrm(13, (N_A_LAYERS, D), 0.02),
        "w_pw2": nrm(14, (N_A_LAYERS, D, D), BETA * D ** -0.5),
        "b_pw2": nrm(15, (N_A_LAYERS, D), 0.01),
        "w_ada_kv": nrm(16, (D, 2 * D), 0.1 * D ** -0.5),
        "b_ada_kv": nrm(17, (2 * D,), 0.01),
        "w_kvf": w_kvf,
        "b_f": 1.0 + 4.0 * jax.random.uniform(ks[23], (H,), f32),
        "w_q": nrm(24, (N_B_LAYERS, D, D), D ** -0.5),
        "w_o": nrm(25, (N_B_LAYERS, D, D), BETA * D ** -0.5),
        "router_w": nrm(26, (D, E), D ** -0.5),
        "router_b": nrm(27, (E,), 0.01),
        "w_gate": nrm(28, (DEPTH, E, D, F), D ** -0.5),
        "w_up": nrm(29, (DEPTH, E, D, F), D ** -0.5),
        "w_down": nrm(30, (DEPTH, E, F, D), BETA * F ** -0.5),
    }


def reference(x, c, w_ada, b_ada, ln_mix_g, ln_mix_b, ln_ffn_g, ln_ffn_b,
              w_pw1, b_pw1, w_dw, b_dw, conv_ln_g, conv_ln_b, w_pw2, b_pw2,
              w_ada_kv, b_ada_kv, w_kvf, b_f, w_q, w_o,
              router_w, router_b, w_gate, w_up, w_down):
    B, S, D = x.shape
    c_act = jax.nn.silu(c)
    k = v = fcum = None
    for l in range(DEPTH):
        mod = (c_act @ w_ada[l] + b_ada[l])[:, None, :]
        sh1, sc1, g1, sh2, sc2, g2 = jnp.split(mod, 6, axis=-1)
        h = x * (1 + sc1) + sh1
        if l < N_A_LAYERS:
            y = conformer_conv(h, w_pw1[l], b_pw1[l], w_dw[l], b_dw[l],
                               conv_ln_g[l], conv_ln_b[l], w_pw2[l], b_pw2[l])
        else:
            if l == N_A_LAYERS:
                k, v, fcum = shared_kv(x, c_act, w_ada_kv, b_ada_kv, w_kvf, b_f)
            j = l - N_A_LAYERS
            q = (h @ w_q[j]).reshape(B, S, N_HEADS, HEAD_DIM)
            y = fox_attention(q, k, v, fcum).reshape(B, S, D) @ w_o[j]
        x = layer_norm(ALPHA * x + (1 + g1) * y, ln_mix_g[l], ln_mix_b[l])
        h = x * (1 + sc2) + sh2
        y = grouped_moe(h, router_w, router_b, w_gate[l], w_up[l], w_down[l])
        x = layer_norm(ALPHA * x + (1 + g2) * y, ln_ffn_g[l], ln_ffn_b[l])
    return x
```

```python
import functools

import jax
import jax.numpy as jnp
from jax import lax
from jax.experimental import pallas as pl
from jax.experimental.pallas import tpu as pltpu

F32 = jnp.float32
BF16 = jnp.bfloat16
U32 = jnp.uint32
I32 = jnp.int32

DEPTH = 4
N_A_LAYERS = 2
N_HEADS = 16
HEAD_DIM = 64
CONV_K = 31
N_EXPERTS = 16
N_GROUPS = 4
EXPERTS_PER_GROUP = 4
ALPHA = (2.0 * DEPTH) ** 0.25
LN_EPS = 1e-5

CONV_HALO = 32
MOE_ROWS = 512
IDX_CHUNK = 1024
IDX_DST = 512
NEG_BIG = -1e30
VMEM_LIMIT = 56 * 1024 * 1024


def _cparams(sem, vmem=VMEM_LIMIT):
    return pltpu.CompilerParams(dimension_semantics=sem, vmem_limit_bytes=vmem)


def _sigmoid(x):
    return 1.0 / (1.0 + jnp.exp(-x))


def _layer_norm(z, g, b):
    mu = jnp.mean(z, axis=-1, keepdims=True)
    zc = z - mu
    var = jnp.mean(zc * zc, axis=-1, keepdims=True)
    return zc * lax.rsqrt(var + LN_EPS) * g + b


def _mod_kernel(c_ref, w_ref, b_ref, o_ref):
    c = c_ref[...]
    ca = (c * _sigmoid(c)).astype(BF16)
    o_ref[0] = jnp.dot(ca, w_ref[0].astype(BF16), preferred_element_type=F32) + b_ref[0]


def _mod_call(c, w, b, tn=1024):
    L, D, M = w.shape
    B = c.shape[0]
    return pl.pallas_call(
        _mod_kernel,
        grid=(L, M // tn),
        in_specs=[
            pl.BlockSpec((B, D), lambda l, j: (0, 0)),
            pl.BlockSpec((1, D, tn), lambda l, j: (l, 0, j)),
            pl.BlockSpec((1, 1, tn), lambda l, j: (l, 0, j)),
        ],
        out_specs=pl.BlockSpec((1, B, tn), lambda l, j: (l, 0, j)),
        out_shape=jax.ShapeDtypeStruct((L, B, M), F32),
        compiler_params=_cparams(("arbitrary", "arbitrary")),
        name="adaln_mod",
    )(c, w, b)


def _modmm_kernel(x_ref, sc_ref, sh_ref, w_ref, b_ref, o_ref, *, glu):
    h = (x_ref[...] * (1.0 + sc_ref[0]) + sh_ref[0]).astype(BF16)
    d_out = o_ref.shape[1]
    if glu:
        a = jnp.dot(h, w_ref[:, :d_out], preferred_element_type=F32) + b_ref[:, :d_out]
        g = jnp.dot(h, w_ref[:, d_out:], preferred_element_type=F32) + b_ref[:, d_out:]
        o_ref[...] = (a * _sigmoid(g)).astype(o_ref.dtype)
    else:
        o = jnp.dot(h, w_ref[...], preferred_element_type=F32) + b_ref[...]
        o_ref[...] = o.astype(o_ref.dtype)


def _modmm_call(x, sc, sh, w, b, seq, *, glu, tm=512, name):
    N, D = x.shape
    M = w.shape[1]
    d_out = M // 2 if glu else M
    per_b = seq // tm
    return pl.pallas_call(
        functools.partial(_modmm_kernel, glu=glu),
        grid=(N // tm,),
        in_specs=[
            pl.BlockSpec((tm, D), lambda i: (i, 0)),
            pl.BlockSpec((1, 1, D), lambda i: (i // per_b, 0, 0)),
            pl.BlockSpec((1, 1, D), lambda i: (i // per_b, 0, 0)),
            pl.BlockSpec((D, M), lambda i: (0, 0)),
            pl.BlockSpec((1, M), lambda i: (0, 0)),
        ],
        out_specs=pl.BlockSpec((tm, d_out), lambda i: (i, 0)),
        out_shape=jax.ShapeDtypeStruct((N, d_out), BF16),
        compiler_params=_cparams(("parallel",)),
        name=name,
    )(x, sc, sh, w, b)


def _conv_kernel(ucur_ref, uprev_ref, wdw_ref, bdw_ref, lg_ref, lb_ref, o_ref, ubuf, ybuf):
    j = pl.program_id(1)
    ts, D = ucur_ref.shape
    keep = (j > 0).astype(F32)
    ubuf[0:CONV_HALO, :] = uprev_ref[...].astype(F32) * keep
    ubuf[CONV_HALO:CONV_HALO + ts, :] = ucur_ref[...].astype(F32)
    R, C = 32, 256
    off = CONV_HALO - (CONV_K - 1)
    for rc in range(ts // R):
        for cc in range(D // C):
            cs = slice(cc * C, (cc + 1) * C)
            acc = jnp.zeros((R, C), F32)
            for k in range(CONV_K):
                r0 = rc * R + off + k
                acc = acc + ubuf[r0:r0 + R, cs] * wdw_ref[k:k + 1, cs]
            ybuf[rc * R:(rc + 1) * R, cs] = acc + bdw_ref[:, cs]
    y = _layer_norm(ybuf[...], lg_ref[...], lb_ref[...])
    o_ref[...] = (y * _sigmoid(y)).astype(o_ref.dtype)


def _conv_call(u, wdw, bdw, lg, lb, batch, seq, ts=256):
    N, D = u.shape
    per_b = seq // ts
    hb = ts // CONV_HALO
    return pl.pallas_call(
        _conv_kernel,
        grid=(batch, per_b),
        in_specs=[
            pl.BlockSpec((ts, D), lambda b, j: (b * per_b + j, 0)),
            pl.BlockSpec((CONV_HALO, D), lambda b, j: (jnp.maximum((b * per_b + j) * hb - 1, 0), 0)),
            pl.BlockSpec((CONV_K, D), lambda b, j: (0, 0)),
            pl.BlockSpec((1, D), lambda b, j: (0, 0)),
            pl.BlockSpec((1, D), lambda b, j: (0, 0)),
            pl.BlockSpec((1, D), lambda b, j: (0, 0)),
        ],
        out_specs=pl.BlockSpec((ts, D), lambda b, j: (b * per_b + j, 0)),
        out_shape=jax.ShapeDtypeStruct((N, D), BF16),
        scratch_shapes=[pltpu.VMEM((CONV_HALO + ts, D), F32), pltpu.VMEM((ts, D), F32)],
        compiler_params=_cparams(("parallel", "parallel")),
        name="dwconv_ln_swish",
    )(u, u, wdw, bdw, lg, lb)


def _route_rows(lt):
    rows = [lt[e:e + 1, :] for e in range(N_EXPERTS)]
    m = rows[0]
    for r in rows[1:]:
        m = jnp.maximum(m, r)
    ex = [jnp.exp(r - m) for r in rows]
    tot = ex[0]
    for r in ex[1:]:
        tot = tot + r
    p = [r / tot for r in ex]
    scores = []
    for g in range(N_GROUPS):
        a, b, c, d = p[4 * g:4 * g + 4]
        hi1, lo1 = jnp.maximum(a, b), jnp.minimum(a, b)
        hi2, lo2 = jnp.maximum(c, d), jnp.minimum(c, d)
        top1 = jnp.maximum(hi1, hi2)
        top2 = jnp.maximum(jnp.minimum(hi1, hi2), jnp.maximum(lo1, lo2))
        scores.append(top1 + top2)
    best = scores[0]
    gi = jnp.zeros_like(best, dtype=I32)
    for g in range(1, N_GROUPS):
        upd = scores[g] > best
        best = jnp.where(upd, scores[g], best)
        gi = jnp.where(upd, g, gi)
    q = []
    for jx in range(EXPERTS_PER_GROUP):
        q.append(jnp.where(gi == 0, p[jx],
                           jnp.where(gi == 1, p[4 + jx], jnp.where(gi == 2, p[8 + jx], p[12 + jx]))))
    v1 = q[0]
    i1 = jnp.zeros_like(gi)
    for jx in range(1, EXPERTS_PER_GROUP):
        upd = q[jx] > v1
        v1 = jnp.where(upd, q[jx], v1)
        i1 = jnp.where(upd, jx, i1)
    qm = [jnp.where(i1 == jx, -1.0, q[jx]) for jx in range(EXPERTS_PER_GROUP)]
    v2 = qm[0]
    i2 = jnp.zeros_like(gi)
    for jx in range(1, EXPERTS_PER_GROUP):
        upd = qm[jx] > v2
        v2 = jnp.where(upd, qm[jx], v2)
        i2 = jnp.where(upd, jx, i2)
    den = v1 + v2
    return gi * EXPERTS_PER_GROUP + i1, gi * EXPERTS_PER_GROUP + i2, v1 / den, v2 / den


def _pack_bf16_pairs(hb):
    half = hb.shape[1] // 2
    bits = lax.bitcast_convert_type(hb.astype(F32), U32)
    return bits[:, :half] | (bits[:, half:] >> 16)


def _unpack_bf16_pairs(xp):
    hi = lax.bitcast_convert_type(xp & jnp.uint32(0xFFFF0000), F32).astype(BF16)
    lo = lax.bitcast_convert_type(xp << 16, F32).astype(BF16)
    return jnp.concatenate([hi, lo], axis=1)


def _post_kernel(v_ref, x_ref, w_ref, b_ref, g1_ref, lg_ref, lb_ref, sc2_ref, sh2_ref, rwt_ref, rb_ref,
                 x1_ref, hp_ref, eid_ref, gw_ref):
    y = jnp.dot(v_ref[...], w_ref[...], preferred_element_type=F32) + b_ref[...]
    z = ALPHA * x_ref[...] + (1.0 + g1_ref[0]) * y
    x1 = _layer_norm(z, lg_ref[...], lb_ref[...])
    x1_ref[...] = x1
    hb = (x1 * (1.0 + sc2_ref[0]) + sh2_ref[0]).astype(BF16)
    hp_ref[...] = _pack_bf16_pairs(hb)
    lt = lax.dot_general(rwt_ref[...], hb, (((1,), (1,)), ((), ())), preferred_element_type=F32) + rb_ref[...]
    e1, e2, w1, w2 = _route_rows(lt)
    eid_ref[0:1, :] = e1
    eid_ref[1:2, :] = e2
    gw_ref[0:1, :] = w1
    gw_ref[1:2, :] = w2


def _post_call(v, x, w, b, g1, lg, lb, sc2, sh2, rwt, rb, seq, tm=512):
    N, D = x.shape
    E = rwt.shape[0]
    per_b = seq // tm
    row = lambda i: (i, 0)
    const = lambda i: (0, 0)
    bat = lambda i: (i // per_b, 0, 0)
    return pl.pallas_call(
        _post_kernel,
        grid=(N // tm,),
        in_specs=[
            pl.BlockSpec((tm, D), row),
            pl.BlockSpec((tm, D), row),
            pl.BlockSpec((D, D), const),
            pl.BlockSpec((1, D), const),
            pl.BlockSpec((1, 1, D), bat),
            pl.BlockSpec((1, D), const),
            pl.BlockSpec((1, D), const),
            pl.BlockSpec((1, 1, D), bat),
            pl.BlockSpec((1, 1, D), bat),
            pl.BlockSpec((E, D), const),
            pl.BlockSpec((E, 1), const),
        ],
        out_specs=[
            pl.BlockSpec((tm, D), row),
            pl.BlockSpec((tm, D // 2), row),
            pl.BlockSpec((2, tm), lambda i: (0, i)),
            pl.BlockSpec((2, tm), lambda i: (0, i)),
        ],
        out_shape=[
            jax.ShapeDtypeStruct((N, D), F32),
            jax.ShapeDtypeStruct((N, D // 2), U32),
            jax.ShapeDtypeStruct((2, N), I32),
            jax.ShapeDtypeStruct((2, N), F32),
        ],
        compiler_params=_cparams(("parallel",)),
        name="mixer_out_ln_route",
    )(v, x, w, b, g1, lg, lb, sc2, sh2, rwt, rb)


def _moe_kernel(bexp_ref, idx_hbm, hp_hbm, wg_ref, wu_ref, wd_ref, ytok_hbm,
                idx_s, xbuf, ybuf, isem, gsem, ssem, *, nblk):
    del bexp_ref
    i = pl.program_id(0)
    slot = i % 2
    other = 1 - slot
    mb = xbuf.shape[1]

    def idx_copy(chunk):
        s4 = chunk % 4
        return pltpu.make_async_copy(idx_hbm.at[pl.ds(chunk * IDX_CHUNK, IDX_CHUNK)], idx_s.at[s4], isem.at[s4])

    def gather_start(chunk, xs):
        s4 = chunk % 4
        for j in range(mb):
            t = idx_s[s4, j]
            pltpu.make_async_copy(hp_hbm.at[pl.ds(t, 1)], xbuf.at[xs, pl.ds(j, 1)], gsem.at[xs]).start()

    def scatter_start(chunk, ys):
        s4 = chunk % 4
        for j in range(mb):
            r = idx_s[s4, IDX_DST + j]
            pltpu.make_async_copy(ybuf.at[ys, pl.ds(j, 1)], ytok_hbm.at[pl.ds(r, 1)], ssem.at[ys]).start()

    def gather_wait(xs):
        pltpu.make_async_copy(hp_hbm.at[pl.ds(0, mb)], xbuf.at[xs], gsem.at[xs]).wait()

    def scatter_wait(ys):
        pltpu.make_async_copy(ybuf.at[ys], ytok_hbm.at[pl.ds(0, mb)], ssem.at[ys]).wait()

    @pl.when(i == 0)
    def _():
        ybuf[1] = jnp.zeros(ybuf.shape[1:], F32)
        for c in range(3):
            idx_copy(c).start()
        for c in range(3):
            idx_copy(c).wait()
        gather_start(1, 0)

    @pl.when(i > 0)
    def _():
        idx_copy(i + 2).wait()

    @pl.when(i + 3 <= nblk + 1)
    def _():
        idx_copy(i + 3).start()

    gather_wait(slot)

    @pl.when(i > 0)
    def _():
        scatter_wait(slot)

    gather_start(i + 2, other)
    scatter_start(i, other)
    xc = _unpack_bf16_pairs(xbuf[slot])
    g = jnp.dot(xc, wg_ref[0], preferred_element_type=F32)
    u = jnp.dot(xc, wu_ref[0], preferred_element_type=F32)
    hm = (g * _sigmoid(g) * u).astype(BF16)
    ybuf[slot] = jnp.dot(hm, wd_ref[0], preferred_element_type=F32)

    @pl.when(i == nblk - 1)
    def _():
        scatter_start(i + 1, slot)
        gather_wait(other)
        scatter_wait(other)
        scatter_wait(slot)


def _moe_call(bexp, idx, hp, wg, wu, wd, n_tok):
    nblk = bexp.shape[0]
    E, D, Fd = wg.shape
    mb = MOE_ROWS
    wspec = lambda shape: pl.BlockSpec(shape, lambda i, be: (be[i], 0, 0))
    grid_spec = pltpu.PrefetchScalarGridSpec(
        num_scalar_prefetch=1,
        grid=(nblk,),
        in_specs=[
            pl.BlockSpec(memory_space=pl.ANY),
            pl.BlockSpec(memory_space=pl.ANY),
            wspec((1, D, Fd)),
            wspec((1, D, Fd)),
            wspec((1, Fd, D)),
        ],
        out_specs=pl.BlockSpec(memory_space=pl.ANY),
        scratch_shapes=[
            pltpu.SMEM((4, IDX_CHUNK), I32),
            pltpu.VMEM((2, mb, D // 2), U32),
            pltpu.VMEM((2, mb, D), F32),
            pltpu.SemaphoreType.DMA((4,)),
            pltpu.SemaphoreType.DMA((2,)),
            pltpu.SemaphoreType.DMA((2,)),
        ],
    )
    return pl.pallas_call(
        functools.partial(_moe_kernel, nblk=nblk),
        grid_spec=grid_spec,
        out_shape=jax.ShapeDtypeStruct((2 * n_tok + 2 * mb, D), F32),
        compiler_params=_cparams(("arbitrary",)),
        name="moe_grouped_mlp",
    )(bexp, idx, hp, wg, wu, wd)


def _route_meta(eid, n_tok):
    mb = MOE_ROWS
    A = 2 * n_tok
    e = eid.reshape(-1)
    order = jnp.argsort(e).astype(I32)
    counts = jnp.bincount(e, length=N_EXPERTS).astype(I32)
    padded = (counts + mb - 1) // mb * mb
    pad_end = jnp.cumsum(padded)
    pad_start = pad_end - padded
    start = jnp.cumsum(counts) - counts
    nblk = A // mb + N_EXPERTS
    blk = jnp.arange(nblk, dtype=I32)
    bexp = jnp.minimum(jnp.searchsorted(pad_end, blk * mb, side="right"), N_EXPERTS - 1).astype(I32)
    j = jnp.arange(mb, dtype=I32)[None, :]
    r = blk[:, None] * mb + j - pad_start[bexp][:, None]
    valid = r < counts[bexp][:, None]
    a = order[jnp.clip(start[bexp][:, None] + r, 0, A - 1)]
    src = jnp.where(valid, jnp.where(a >= n_tok, a - n_tok, a), 0)
    dummy = A + (blk[:, None] % 2) * mb + j
    dst = jnp.where(valid, a, dummy)
    chunk = jnp.zeros((nblk + 2, IDX_CHUNK), I32)
    chunk = chunk.at[:, IDX_DST:IDX_DST + mb].set(jnp.broadcast_to(A + mb + j, (nblk + 2, mb)))
    chunk = chunk.at[1:nblk + 1, :mb].set(src)
    chunk = chunk.at[1:nblk + 1, IDX_DST:IDX_DST + mb].set(dst)
    return bexp, chunk.reshape(-1)


def _combine_kernel(x1_ref, y0_ref, y1_ref, gw_ref, g2_ref, lg_ref, lb_ref, o_ref):
    tm = x1_ref.shape[0]
    gw = gw_ref[...]
    eye = lax.broadcasted_iota(I32, (tm, tm), 0) == lax.broadcasted_iota(I32, (tm, tm), 1)
    w0 = jnp.sum(jnp.where(eye, jnp.broadcast_to(gw[0:1, :], (tm, tm)), 0.0), axis=1, keepdims=True)
    w1 = jnp.sum(jnp.where(eye, jnp.broadcast_to(gw[1:2, :], (tm, tm)), 0.0), axis=1, keepdims=True)
    y = w0 * y0_ref[...] + w1 * y1_ref[...]
    z = ALPHA * x1_ref[...] + (1.0 + g2_ref[0]) * y
    o_ref[...] = _layer_norm(z, lg_ref[...], lb_ref[...])


def _combine_call(x1, ytok, gw, g2, lg, lb, seq, tm=256):
    N, D = x1.shape
    per_b = seq // tm
    nt = N // tm
    const = lambda i: (0, 0)
    return pl.pallas_call(
        _combine_kernel,
        grid=(nt,),
        in_specs=[
            pl.BlockSpec((tm, D), lambda i: (i, 0)),
            pl.BlockSpec((tm, D), lambda i: (i, 0)),
            pl.BlockSpec((tm, D), lambda i: (nt + i, 0)),
            pl.BlockSpec((2, tm), lambda i: (0, i)),
            pl.BlockSpec((1, 1, D), lambda i: (i // per_b, 0, 0)),
            pl.BlockSpec((1, D), const),
            pl.BlockSpec((1, D), const),
        ],
        out_specs=pl.BlockSpec((tm, D), lambda i: (i, 0)),
        out_shape=jax.ShapeDtypeStruct((N, D), F32),
        compiler_params=_cparams(("parallel",)),
        name="moe_combine_ln",
    )(x1, ytok, ytok, gw, g2, lg, lb)


def _kvf_kernel(x_ref, sc_ref, sh_ref, wk_ref, wv_ref, wf_ref, bf_ref, k_ref, v_ref, f_ref, tri_ref, carry_ref,
                *, per_b):
    i = pl.program_id(0)
    tm = x_ref.shape[0]

    @pl.when(i == 0)
    def _():
        r = lax.broadcasted_iota(I32, (tm, tm), 0)
        c = lax.broadcasted_iota(I32, (tm, tm), 1)
        tri_ref[...] = jnp.where(r >= c, 1.0, 0.0).astype(BF16)

    @pl.when(i % per_b == 0)
    def _():
        carry_ref[...] = jnp.zeros(carry_ref.shape, F32)

    h = (x_ref[...] * (1.0 + sc_ref[0]) + sh_ref[0]).astype(BF16)
    k_ref[...] = jnp.dot(h, wk_ref[...], preferred_element_type=F32).astype(BF16)
    v_ref[...] = jnp.dot(h, wv_ref[...], preferred_element_type=F32).astype(BF16)
    z = jnp.dot(h, wf_ref[...], preferred_element_type=F32) + bf_ref[...]
    lf = jnp.minimum(z, 0.0) - jnp.log(1.0 + jnp.exp(-jnp.abs(z)))
    p1 = lf.astype(BF16)
    r1 = lf - p1.astype(F32)
    p2 = r1.astype(BF16)
    p3 = (r1 - p2.astype(F32)).astype(BF16)
    tri = tri_ref[...]
    cs = (jnp.dot(tri, p1, preferred_element_type=F32) + jnp.dot(tri, p2, preferred_element_type=F32)
          + jnp.dot(tri, p3, preferred_element_type=F32)) + carry_ref[...]
    carry_ref[...] = cs[tm - 1:tm, :]
    f_ref[...] = cs[:, :N_HEADS]


def _kvf_call(x, sc, sh, wk, wv, wf, bf, seq, tm=512):
    N, D = x.shape
    per_b = seq // tm
    const = lambda i: (0, 0)
    bat = lambda i: (i // per_b, 0, 0)
    row = lambda i: (i, 0)
    FP = wf.shape[1]
    return pl.pallas_call(
        functools.partial(_kvf_kernel, per_b=per_b),
        grid=(N // tm,),
        in_specs=[
            pl.BlockSpec((tm, D), row),
            pl.BlockSpec((1, 1, D), bat),
            pl.BlockSpec((1, 1, D), bat),
            pl.BlockSpec((D, D), const),
            pl.BlockSpec((D, D), const),
            pl.BlockSpec((D, FP), const),
            pl.BlockSpec((1, FP), const),
        ],
        out_specs=[
            pl.BlockSpec((tm, D), row),
            pl.BlockSpec((tm, D), row),
            pl.BlockSpec((tm, N_HEADS), row),
        ],
        out_shape=[
            jax.ShapeDtypeStruct((N, D), BF16),
            jax.ShapeDtypeStruct((N, D), BF16),
            jax.ShapeDtypeStruct((N, N_HEADS), F32),
        ],
        scratch_shapes=[pltpu.VMEM((tm, tm), BF16), pltpu.VMEM((1, FP), F32)],
        compiler_params=_cparams(("arbitrary",)),
        name="shared_kv_forget",
    )(x, sc, sh, wk, wv, wf, bf)


def _attn_kernel(q_ref, k_ref, v_ref, fq_ref, fk_ref, o_ref, m_ref, l_ref, acc_ref, *, scale):
    hp = pl.program_id(1)
    qi = pl.program_id(2)
    tq, W = q_ref.shape
    q = q_ref[...]
    lane = lax.broadcasted_iota(I32, (tq, W), 1)
    qh = (jnp.where(lane < HEAD_DIM, q, jnp.zeros_like(q)), jnp.where(lane >= HEAD_DIM, q, jnp.zeros_like(q)))
    fqb = fq_ref[...]
    lane_h = lax.broadcasted_iota(I32, fqb.shape, 1)
    fq = tuple(jnp.sum(jnp.where(lane_h == 2 * hp + c, fqb, 0.0), axis=1, keepdims=True) for c in range(2))
    m_ref[...] = jnp.full(m_ref.shape, NEG_BIG, F32)
    l_ref[...] = jnp.zeros(l_ref.shape, F32)
    acc_ref[...] = jnp.zeros(acc_ref.shape, F32)

    def tile(j, masked):
        k0 = pl.multiple_of(j * tq, tq)
        kt = k_ref[pl.ds(k0, tq), :]
        vt = v_ref[pl.ds(k0, tq), :]
        fk = fk_ref[0, 0, :, pl.ds(k0, tq)]
        for c in range(2):
            s = lax.dot_general(qh[c], kt, (((1,), (1,)), ((), ())), preferred_element_type=F32)
            s = s * scale + (fq[c] - fk[c:c + 1, :])
            if masked:
                rr = lax.broadcasted_iota(I32, (tq, tq), 0)
                cc = lax.broadcasted_iota(I32, (tq, tq), 1)
                s = jnp.where(rr >= cc, s, NEG_BIG)
            m_old = m_ref[c]
            m_new = jnp.maximum(m_old, jnp.max(s, axis=1, keepdims=True))
            p = jnp.exp(s - m_new)
            a = jnp.exp(m_old - m_new)
            l_ref[c] = a * l_ref[c] + jnp.sum(p, axis=1, keepdims=True)
            acc_ref[c] = a * acc_ref[c] + jnp.dot(p.astype(BF16), vt, preferred_element_type=F32)
            m_ref[c] = m_new

    def body(j, carry):
        tile(j, False)
        return carry

    lax.fori_loop(0, qi, body, 0)
    tile(qi, True)
    o0 = acc_ref[0] / l_ref[0]
    o1 = acc_ref[1] / l_ref[1]
    o_ref[...] = jnp.where(lane < HEAD_DIM, o0, o1).astype(o_ref.dtype)


def _attn_call(q, k, v, fq, fk, batch, seq, tq=256):
    N, D = q.shape
    W = 2 * HEAD_DIM
    nhp = D // W
    nq = seq // tq
    return pl.pallas_call(
        functools.partial(_attn_kernel, scale=HEAD_DIM ** -0.5),
        grid=(batch, nhp, nq),
        in_specs=[
            pl.BlockSpec((tq, W), lambda b, h, i: (b * nq + i, h)),
            pl.BlockSpec((seq, W), lambda b, h, i: (b, h)),
            pl.BlockSpec((seq, W), lambda b, h, i: (b, h)),
            pl.BlockSpec((tq, N_HEADS), lambda b, h, i: (b * nq + i, 0)),
            pl.BlockSpec((1, 1, 2, seq), lambda b, h, i: (b, h, 0, 0)),
        ],
        out_specs=pl.BlockSpec((tq, W), lambda b, h, i: (b * nq + i, h)),
        out_shape=jax.ShapeDtypeStruct((N, D), BF16),
        scratch_shapes=[
            pltpu.VMEM((2, tq, 1), F32),
            pltpu.VMEM((2, tq, 1), F32),
            pltpu.VMEM((2, tq, W), F32),
        ],
        compiler_params=_cparams(("parallel", "parallel", "arbitrary")),
        name="fox_attention",
    )(q, k, v, fq, fk)


def kernel(x, c, w_ada, b_ada, ln_mix_g, ln_mix_b, ln_ffn_g, ln_ffn_b, w_pw1, b_pw1, w_dw, b_dw, conv_ln_g,
           conv_ln_b, w_pw2, b_pw2, w_ada_kv, b_ada_kv, w_kvf, b_f, w_q, w_o, router_w, router_b, w_gate, w_up,
           w_down):
    B, S, D = x.shape
    N = B * S
    xf = x.reshape(N, D)

    mod = _mod_call(c, w_ada, b_ada[:, None, :])
    mod_kv = _mod_call(c, w_ada_kv[None], b_ada_kv[None, None, :])

    def mvec(l, idx):
        return mod[l, :, idx * D:(idx + 1) * D][:, None, :]

    rwt = router_w.T.astype(BF16)
    rb = router_b[:, None].astype(F32)
    zero_b = jnp.zeros((1, D), F32)
    k = v = fq = fk = None

    for l in range(DEPTH):
        sh1, sc1, g1, sh2, sc2, g2 = (mvec(l, t) for t in range(6))
        if l < N_A_LAYERS:
            u = _modmm_call(xf, sc1, sh1, w_pw1[l].astype(BF16), b_pw1[l][None, :], S, glu=True, name="pw1_glu")
            mix = _conv_call(u, w_dw[l][:, 0, :], b_dw[l][None, :], conv_ln_g[l][None, :], conv_ln_b[l][None, :],
                             B, S)
            w_out, b_out = w_pw2[l].astype(BF16), b_pw2[l][None, :]
        else:
            jl = l - N_A_LAYERS
            if l == N_A_LAYERS:
                wf = jnp.zeros((D, 128), F32).at[:, :N_HEADS].set(w_kvf[:, 2 * D:]).astype(BF16)
                bf = jnp.zeros((1, 128), F32).at[0, :N_HEADS].set(b_f)
                k, v, fq = _kvf_call(xf, mod_kv[0, :, D:][:, None, :], mod_kv[0, :, :D][:, None, :],
                                     w_kvf[:, :D].astype(BF16), w_kvf[:, D:2 * D].astype(BF16), wf, bf, S)
                fk = fq.reshape(B, S, N_HEADS // 2, 2).transpose(0, 2, 3, 1)
            q = _modmm_call(xf, sc1, sh1, w_q[jl].astype(BF16), zero_b, S, glu=False, name="q_proj")
            mix = _attn_call(q, k, v, fq, fk, B, S)
            w_out, b_out = w_o[jl].astype(BF16), zero_b
        x1, hp, eid, gw = _post_call(mix, xf, w_out, b_out, g1, ln_mix_g[l][None, :], ln_mix_b[l][None, :],
                                     sc2, sh2, rwt, rb, S)
        bexp, idx = _route_meta(eid, N)
        ytok = _moe_call(bexp, idx, hp, w_gate[l].astype(BF16), w_up[l].astype(BF16), w_down[l].astype(BF16), N)
        xf = _combine_call(x1, ytok, gw, g2, ln_ffn_g[l][None, :], ln_ffn_b[l][None, :], S)
    return xf.reshape(B, S, D)
```

```python
import functools
import math

import numpy as np
import jax
import jax.numpy as jnp
from jax import lax
from jax.experimental import pallas as pl
from jax.experimental.pallas import tpu as pltpu

F32 = jnp.float32
BF16 = jnp.bfloat16
I32 = jnp.int32

DEPTH = 4
N_A_LAYERS = 2
N_HEADS = 16
HEAD_DIM = 64
CONV_K = 31
N_EXPERTS = 16
N_GROUPS = 4
EXPERTS_PER_GROUP = 4
ALPHA = (2.0 * DEPTH) ** 0.25
LN_EPS = 1e-5
LOG2E = math.log2(math.e)

LANES = 128
SUBLANES = 8
CONV_HALO = 32
MOE_ROWS = 512
MOE_SEGMENTS = 8
DMA_BATCH = 8
IDX_CHUNK = 1024
IDX_DST = 512
HEAD_W = LANES
AUG = HEAD_DIM
ATTN_HEADS = 4
NEG_BIG = -1e30
VMEM_LIMIT = 56 * 1024 * 1024


def _cparams(sem, vmem=VMEM_LIMIT):
    return pltpu.CompilerParams(dimension_semantics=sem, vmem_limit_bytes=vmem)


def _sigmoid(x):
    return 1.0 / (1.0 + jnp.exp(-x))


def _layer_norm(z, g, b):
    mu = jnp.mean(z, axis=-1, keepdims=True)
    zc = z - mu
    var = jnp.mean(zc * zc, axis=-1, keepdims=True)
    return zc * lax.rsqrt(var + LN_EPS) * g + b


def _rows_to_tiles(ref, val):
    t = val.shape[0]
    for s in range(val.shape[1] // LANES):
        ref[pl.ds(s, t, stride=SUBLANES), :] = val[:, s * LANES:(s + 1) * LANES]


def _tiles_to_rows(ref, t):
    return jnp.concatenate([ref[pl.ds(s, t, stride=SUBLANES), :] for s in range(SUBLANES)], axis=1)


def _mod_kernel(c_ref, w_ref, b_ref, o_ref):
    c = c_ref[...]
    ca = (c * _sigmoid(c)).astype(BF16)
    o_ref[0] = jnp.dot(ca, w_ref[0].astype(BF16), preferred_element_type=F32) + b_ref[0]


def _mod_call(c, w, b, tn=1024):
    L, D, M = w.shape
    B = c.shape[0]
    return pl.pallas_call(
        _mod_kernel,
        grid=(L, M // tn),
        in_specs=[
            pl.BlockSpec((B, D), lambda l, j: (0, 0)),
            pl.BlockSpec((1, D, tn), lambda l, j: (l, 0, j)),
            pl.BlockSpec((1, 1, tn), lambda l, j: (l, 0, j)),
        ],
        out_specs=pl.BlockSpec((1, B, tn), lambda l, j: (l, 0, j)),
        out_shape=jax.ShapeDtypeStruct((L, B, M), F32),
        compiler_params=_cparams(("arbitrary", "arbitrary")),
        name="adaln_mod",
    )(c, w, b)


def _glu_kernel(x_ref, sc_ref, sh_ref, w_ref, b_ref, o_ref):
    h = (x_ref[...] * (1.0 + sc_ref[0]) + sh_ref[0]).astype(BF16)
    d_out = o_ref.shape[1]
    a = jnp.dot(h, w_ref[:, :d_out], preferred_element_type=F32) + b_ref[:, :d_out]
    g = jnp.dot(h, w_ref[:, d_out:], preferred_element_type=F32) + b_ref[:, d_out:]
    o_ref[...] = (a * _sigmoid(g)).astype(o_ref.dtype)


def _glu_call(x, sc, sh, w, b, seq, tm=512):
    N, D = x.shape
    M = w.shape[1]
    per_b = seq // tm
    return pl.pallas_call(
        _glu_kernel,
        grid=(N // tm,),
        in_specs=[
            pl.BlockSpec((tm, D), lambda i: (i, 0)),
            pl.BlockSpec((1, 1, D), lambda i: (i // per_b, 0, 0)),
            pl.BlockSpec((1, 1, D), lambda i: (i // per_b, 0, 0)),
            pl.BlockSpec((D, M), lambda i: (0, 0)),
            pl.BlockSpec((1, M), lambda i: (0, 0)),
        ],
        out_specs=pl.BlockSpec((tm, M // 2), lambda i: (i, 0)),
        out_shape=jax.ShapeDtypeStruct((N, M // 2), BF16),
        compiler_params=_cparams(("parallel",)),
        name="pw1_glu",
    )(x, sc, sh, w, b)


def _conv_kernel(ucur_ref, uprev_ref, wdw_ref, bdw_ref, lg_ref, lb_ref, o_ref, ubuf, ybuf):
    j = pl.program_id(1)
    ts, D = ucur_ref.shape
    keep = jnp.where(j > 0, 1.0, 0.0)
    ubuf[0:CONV_HALO, :] = uprev_ref[...].astype(F32) * keep
    ubuf[CONV_HALO:CONV_HALO + ts, :] = ucur_ref[...].astype(F32)
    R, C = 64, LANES
    off = CONV_HALO - (CONV_K - 1)
    for rc in range(ts // R):
        base = rc * R
        for cc in range(D // C):
            cs = slice(cc * C, (cc + 1) * C)
            y = None
            for b in range(SUBLANES):
                rows = R if b == 0 else R + SUBLANES
                z = None
                for a in range((off + CONV_K - 1) // SUBLANES + 1):
                    o = SUBLANES * a + b
                    if o < off or o >= off + CONV_K:
                        continue
                    k = o - off
                    r0 = base + SUBLANES * a
                    term = ubuf[r0:r0 + rows, cs] * wdw_ref[k:k + 1, cs]
                    z = term if z is None else z + term
                zs = z[b:b + R]
                y = zs if y is None else y + zs
            ybuf[base:base + R, cs] = y + bdw_ref[:, cs]
    y = _layer_norm(ybuf[...], lg_ref[...], lb_ref[...])
    o_ref[...] = (y * _sigmoid(y)).astype(o_ref.dtype)


def _conv_call(u, wdw, bdw, lg, lb, batch, seq, ts=256):
    N, D = u.shape
    per_b = seq // ts
    hb = ts // CONV_HALO
    return pl.pallas_call(
        _conv_kernel,
        grid=(batch, per_b),
        in_specs=[
            pl.BlockSpec((ts, D), lambda b, j: (b * per_b + j, 0)),
            pl.BlockSpec((CONV_HALO, D), lambda b, j: (jnp.maximum((b * per_b + j) * hb - 1, 0), 0)),
            pl.BlockSpec((CONV_K, D), lambda b, j: (0, 0)),
            pl.BlockSpec((1, D), lambda b, j: (0, 0)),
            pl.BlockSpec((1, D), lambda b, j: (0, 0)),
            pl.BlockSpec((1, D), lambda b, j: (0, 0)),
        ],
        out_specs=pl.BlockSpec((ts, D), lambda b, j: (b * per_b + j, 0)),
        out_shape=jax.ShapeDtypeStruct((N, D), BF16),
        scratch_shapes=[pltpu.VMEM((CONV_HALO + ts, D), F32), pltpu.VMEM((ts, D), F32)],
        compiler_params=_cparams(("parallel", "parallel")),
        name="dwconv_ln_swish",
    )(u, u, wdw, bdw, lg, lb)


def _route_rows(lt):
    rows = [lt[e:e + 1, :] for e in range(N_EXPERTS)]
    m = rows[0]
    for r in rows[1:]:
        m = jnp.maximum(m, r)
    ex = [jnp.exp(r - m) for r in rows]
    tot = ex[0]
    for r in ex[1:]:
        tot = tot + r
    p = [r / tot for r in ex]
    scores = []
    for g in range(N_GROUPS):
        a, b, c, d = p[4 * g:4 * g + 4]
        hi1, lo1 = jnp.maximum(a, b), jnp.minimum(a, b)
        hi2, lo2 = jnp.maximum(c, d), jnp.minimum(c, d)
        top1 = jnp.maximum(hi1, hi2)
        top2 = jnp.maximum(jnp.minimum(hi1, hi2), jnp.maximum(lo1, lo2))
        scores.append(top1 + top2)
    best = scores[0]
    gi = jnp.zeros_like(best, dtype=I32)
    for g in range(1, N_GROUPS):
        upd = scores[g] > best
        best = jnp.where(upd, scores[g], best)
        gi = jnp.where(upd, g, gi)
    q = []
    for jx in range(EXPERTS_PER_GROUP):
        q.append(jnp.where(gi == 0, p[jx],
                           jnp.where(gi == 1, p[4 + jx], jnp.where(gi == 2, p[8 + jx], p[12 + jx]))))
    v1 = q[0]
    i1 = jnp.zeros_like(gi)
    for jx in range(1, EXPERTS_PER_GROUP):
        upd = q[jx] > v1
        v1 = jnp.where(upd, q[jx], v1)
        i1 = jnp.where(upd, jx, i1)
    qm = [jnp.where(i1 == jx, -1.0, q[jx]) for jx in range(EXPERTS_PER_GROUP)]
    v2 = qm[0]
    i2 = jnp.zeros_like(gi)
    for jx in range(1, EXPERTS_PER_GROUP):
        upd = qm[jx] > v2
        v2 = jnp.where(upd, qm[jx], v2)
        i2 = jnp.where(upd, jx, i2)
    den = v1 + v2
    return gi * EXPERTS_PER_GROUP + i1, gi * EXPERTS_PER_GROUP + i2, v1 / den, v2 / den


def _post_kernel(v_ref, x_ref, w_ref, b_ref, g1_ref, lg_ref, lb_ref, sc2_ref, sh2_ref, rwt_ref, rb_ref,
                 x1_ref, h_ref, eid_ref, gw_ref):
    y = jnp.dot(v_ref[...], w_ref[...], preferred_element_type=F32) + b_ref[...]
    z = ALPHA * x_ref[...] + (1.0 + g1_ref[0]) * y
    x1 = _layer_norm(z, lg_ref[...], lb_ref[...])
    x1_ref[...] = x1
    h = x1 * (1.0 + sc2_ref[0]) + sh2_ref[0]
    _rows_to_tiles(h_ref, h)
    lt = lax.dot_general(rwt_ref[...], h.astype(BF16), (((1,), (1,)), ((), ())),
                         preferred_element_type=F32) + rb_ref[...]
    e1, e2, w1, w2 = _route_rows(lt)
    eid_ref[0:1, :] = e1
    eid_ref[1:2, :] = e2
    gw_ref[0:1, :] = w1
    gw_ref[1:2, :] = w2


def _post_call(v, x, w, b, g1, lg, lb, sc2, sh2, rwt, rb, seq, tm=512):
    N, D = x.shape
    Dv = v.shape[1]
    E = rwt.shape[0]
    per_b = seq // tm
    row = lambda i: (i, 0)
    const = lambda i: (0, 0)
    bat = lambda i: (i // per_b, 0, 0)
    return pl.pallas_call(
        _post_kernel,
        grid=(N // tm,),
        in_specs=[
            pl.BlockSpec((tm, Dv), row),
            pl.BlockSpec((tm, D), row),
            pl.BlockSpec((Dv, D), const),
            pl.BlockSpec((1, D), const),
            pl.BlockSpec((1, 1, D), bat),
            pl.BlockSpec((1, D), const),
            pl.BlockSpec((1, D), const),
            pl.BlockSpec((1, 1, D), bat),
            pl.BlockSpec((1, 1, D), bat),
            pl.BlockSpec((E, D), const),
            pl.BlockSpec((E, 1), const),
        ],
        out_specs=[
            pl.BlockSpec((tm, D), row),
            pl.BlockSpec((tm * SUBLANES, LANES), row),
            pl.BlockSpec((2, tm), lambda i: (0, i)),
            pl.BlockSpec((2, tm), lambda i: (0, i)),
        ],
        out_shape=[
            jax.ShapeDtypeStruct((N, D), F32),
            jax.ShapeDtypeStruct((N * SUBLANES, LANES), F32),
            jax.ShapeDtypeStruct((2, N), I32),
            jax.ShapeDtypeStruct((2, N), F32),
        ],
        compiler_params=_cparams(("parallel",)),
        name="mixer_out_ln_route",
    )(v, x, w, b, g1, lg, lb, sc2, sh2, rwt, rb)


def _moe_kernel(bexp_ref, idx_hbm, h_hbm, wg_ref, wu_ref, wd_ref, ytok_hbm,
                idx_s, xbuf, ybuf, hm_ref, isem, gsem, ssem, *, nblk):
    del bexp_ref
    i = pl.program_id(0)
    slot = i % 2
    other = 1 - slot
    mb = hm_ref.shape[0]
    fd = hm_ref.shape[1]

    def tile(r):
        if isinstance(r, int):
            return pl.ds(r * SUBLANES, SUBLANES)
        return pl.ds(pl.multiple_of(r, SUBLANES), SUBLANES)

    def idx_copy(chunk):
        s4 = chunk % 4
        return pltpu.make_async_copy(idx_hbm.at[pl.ds(chunk * IDX_CHUNK, IDX_CHUNK)], idx_s.at[s4], isem.at[s4])

    def gather_start(chunk, xs, lo, hi):
        s4 = chunk % 4
        for j0 in range(lo, hi, DMA_BATCH):
            toks = [idx_s[s4, j] for j in range(j0, min(j0 + DMA_BATCH, hi))]
            for dj, t in enumerate(toks):
                pltpu.make_async_copy(h_hbm.at[tile(t)], xbuf.at[xs, tile(j0 + dj)], gsem.at[xs]).start()

    def scatter_start(chunk, ys, lo, hi):
        s4 = chunk % 4
        for j0 in range(lo, hi, DMA_BATCH):
            rows = [idx_s[s4, IDX_DST + j] for j in range(j0, min(j0 + DMA_BATCH, hi))]
            for dj, r in enumerate(rows):
                pltpu.make_async_copy(ybuf.at[ys, tile(j0 + dj)], ytok_hbm.at[tile(r)], ssem.at[ys]).start(priority=1)

    def gather_wait(xs):
        pltpu.make_async_copy(h_hbm.at[pl.ds(0, mb * SUBLANES)], xbuf.at[xs], gsem.at[xs]).wait()

    def scatter_wait(ys):
        pltpu.make_async_copy(ybuf.at[ys], ytok_hbm.at[pl.ds(0, mb * SUBLANES)], ssem.at[ys]).wait()

    @pl.when(i == 0)
    def _():
        ybuf[1] = jnp.zeros(ybuf.shape[1:], F32)
        for c in range(3):
            idx_copy(c).start()
        for c in range(3):
            idx_copy(c).wait()
        gather_start(1, 0, 0, mb)

    @pl.when(i > 0)
    def _():
        idx_copy(i + 2).wait()

    @pl.when(i + 3 <= nblk + 1)
    def _():
        idx_copy(i + 3).start()

    gather_wait(slot)

    @pl.when(i > 0)
    def _():
        scatter_wait(slot)

    half = MOE_SEGMENTS // 2
    per_seg = mb // half
    nc = fd // half
    xc = _tiles_to_rows(xbuf.at[slot], mb).astype(BF16)
    for sg in range(MOE_SEGMENTS):
        if sg < half:
            gather_start(i + 2, other, sg * per_seg, (sg + 1) * per_seg)
            scatter_start(i, other, sg * per_seg, (sg + 1) * per_seg)
            cs = slice(sg * nc, (sg + 1) * nc)
            g = jnp.dot(xc, wg_ref[0, :, cs], preferred_element_type=F32)
            u = jnp.dot(xc, wu_ref[0, :, cs], preferred_element_type=F32)
            hm_ref[:, cs] = (g * _sigmoid(g) * u).astype(BF16)
        else:
            n = sg - half
            dn = wd_ref.shape[2] // half
            y = jnp.dot(hm_ref[...], wd_ref[0, :, n * dn:(n + 1) * dn], preferred_element_type=F32)
            for s in range(dn // LANES):
                ybuf[slot, pl.ds(n * (dn // LANES) + s, mb, stride=SUBLANES), :] = y[:, s * LANES:(s + 1) * LANES]

    @pl.when(i == nblk - 1)
    def _():
        scatter_start(i + 1, slot, 0, mb)
        gather_wait(other)
        scatter_wait(other)
        scatter_wait(slot)


def _moe_call(bexp, idx, h3, wg, wu, wd, n_tok):
    nblk = bexp.shape[0]
    E, D, Fd = wg.shape
    mb = MOE_ROWS
    wspec = lambda shape: pl.BlockSpec(shape, lambda i, be: (be[i], 0, 0))
    grid_spec = pltpu.PrefetchScalarGridSpec(
        num_scalar_prefetch=1,
        grid=(nblk,),
        in_specs=[
            pl.BlockSpec(memory_space=pl.ANY),
            pl.BlockSpec(memory_space=pl.ANY),
            wspec((1, D, Fd)),
            wspec((1, D, Fd)),
            wspec((1, Fd, D)),
        ],
        out_specs=pl.BlockSpec(memory_space=pl.ANY),
        scratch_shapes=[
            pltpu.SMEM((4, IDX_CHUNK), I32),
            pltpu.VMEM((2, mb * SUBLANES, LANES), F32),
            pltpu.VMEM((2, mb * SUBLANES, LANES), F32),
            pltpu.VMEM((mb, Fd), BF16),
            pltpu.SemaphoreType.DMA((4,)),
            pltpu.SemaphoreType.DMA((2,)),
            pltpu.SemaphoreType.DMA((2,)),
        ],
    )
    return pl.pallas_call(
        functools.partial(_moe_kernel, nblk=nblk),
        grid_spec=grid_spec,
        out_shape=jax.ShapeDtypeStruct(((2 * n_tok + 2 * mb) * SUBLANES, LANES), F32),
        compiler_params=_cparams(("arbitrary",)),
        name="moe_grouped_mlp",
    )(bexp, idx, h3, wg, wu, wd)


def _route_meta(eid, n_tok):
    mb = MOE_ROWS
    A = 2 * n_tok
    e = eid.reshape(-1)
    order = jnp.argsort(e).astype(I32)
    counts = jnp.bincount(e, length=N_EXPERTS).astype(I32)
    padded = (counts + mb - 1) // mb * mb
    pad_end = jnp.cumsum(padded)
    pad_start = pad_end - padded
    start = jnp.cumsum(counts) - counts
    nblk = A // mb + N_EXPERTS
    blk = jnp.arange(nblk, dtype=I32)
    bexp = jnp.minimum(jnp.searchsorted(pad_end, blk * mb, side="right"), N_EXPERTS - 1).astype(I32)
    j = jnp.arange(mb, dtype=I32)[None, :]
    r = blk[:, None] * mb + j - pad_start[bexp][:, None]
    valid = r < counts[bexp][:, None]
    a = order[jnp.clip(start[bexp][:, None] + r, 0, A - 1)]
    src = jnp.where(valid, jnp.where(a >= n_tok, a - n_tok, a), 0)
    dummy = A + (blk[:, None] % 2) * mb + j
    dst = jnp.where(valid, a, dummy)
    chunk = jnp.zeros((nblk + 2, IDX_CHUNK), I32)
    chunk = chunk.at[:, IDX_DST:IDX_DST + mb].set(jnp.broadcast_to(A + mb + j, (nblk + 2, mb)))
    chunk = chunk.at[1:nblk + 1, :mb].set(src)
    chunk = chunk.at[1:nblk + 1, IDX_DST:IDX_DST + mb].set(dst)
    return bexp, (chunk * SUBLANES).reshape(-1)


def _combine_kernel(x1_ref, y0_ref, y1_ref, gw_ref, g2_ref, lg_ref, lb_ref, o_ref):
    tm = x1_ref.shape[0]
    gw = gw_ref[...]
    eye = lax.broadcasted_iota(I32, (tm, tm), 0) == lax.broadcasted_iota(I32, (tm, tm), 1)
    w0 = jnp.sum(jnp.where(eye, jnp.broadcast_to(gw[0:1, :], (tm, tm)), 0.0), axis=1, keepdims=True)
    w1 = jnp.sum(jnp.where(eye, jnp.broadcast_to(gw[1:2, :], (tm, tm)), 0.0), axis=1, keepdims=True)
    y = w0 * _tiles_to_rows(y0_ref, tm) + w1 * _tiles_to_rows(y1_ref, tm)
    z = ALPHA * x1_ref[...] + (1.0 + g2_ref[0]) * y
    o_ref[...] = _layer_norm(z, lg_ref[...], lb_ref[...])


def _combine_call(x1, ytok, gw, g2, lg, lb, seq, tm=256):
    N, D = x1.shape
    per_b = seq // tm
    nt = N // tm
    const = lambda i: (0, 0)
    return pl.pallas_call(
        _combine_kernel,
        grid=(nt,),
        in_specs=[
            pl.BlockSpec((tm, D), lambda i: (i, 0)),
            pl.BlockSpec((tm * SUBLANES, LANES), lambda i: (i, 0)),
            pl.BlockSpec((tm * SUBLANES, LANES), lambda i: (nt + i, 0)),
            pl.BlockSpec((2, tm), lambda i: (0, i)),
            pl.BlockSpec((1, 1, D), lambda i: (i // per_b, 0, 0)),
            pl.BlockSpec((1, D), const),
            pl.BlockSpec((1, D), const),
        ],
        out_specs=pl.BlockSpec((tm, D), lambda i: (i, 0)),
        out_shape=jax.ShapeDtypeStruct((N, D), F32),
        compiler_params=_cparams(("parallel",)),
        name="moe_combine_ln",
    )(x1, ytok, ytok, gw, g2, lg, lb)


def _aug_constants():
    hw = N_HEADS * HEAD_W
    place_k = np.zeros((LANES, hw), np.float32)
    place_q = np.zeros((LANES, hw), np.float32)
    const_k = np.zeros((1, hw), np.float32)
    const_q = np.zeros((1, hw), np.float32)
    const_v = np.zeros((hw, 1), np.float32)
    for h in range(N_HEADS):
        for part in range(3):
            place_k[part * N_HEADS + h, h * HEAD_W + AUG + part] = -1.0
            place_q[part * N_HEADS + h, h * HEAD_W + AUG + 3 + part] = 1.0
            const_q[0, h * HEAD_W + AUG + part] = 1.0
            const_k[0, h * HEAD_W + AUG + 3 + part] = 1.0
        const_v[h * HEAD_W + (AUG if h % 2 == 0 else 0), 0] = 1.0
    return (jnp.asarray(place_k, BF16), jnp.asarray(place_q, BF16), jnp.asarray(const_k), jnp.asarray(const_q),
            jnp.asarray(const_v))


def _spread_heads(w, odd_high):
    D = w.shape[0]
    w4 = w.reshape(D, N_HEADS // 2, 2, HEAD_DIM)
    z = jnp.zeros_like(w4[:, :, 0])
    even = jnp.concatenate([w4[:, :, 0], z], axis=-1)
    odd = jnp.concatenate([z, w4[:, :, 1]], axis=-1) if odd_high else jnp.concatenate([w4[:, :, 1], z], axis=-1)
    return jnp.stack([even, odd], axis=2).reshape(D, N_HEADS * HEAD_W)


def _kvf_kernel(x_ref, sc_ref, sh_ref, wk_ref, wv_ref, wf_ref, bf_ref, pk_ref, ck_ref, cv_ref,
                k_ref, v_ref, g_ref, tri_ref, carry_ref, *, per_b):
    i = pl.program_id(0)
    tm = x_ref.shape[0]

    @pl.when(i == 0)
    def _():
        r = lax.broadcasted_iota(I32, (tm, tm), 0)
        c = lax.broadcasted_iota(I32, (tm, tm), 1)
        tri_ref[...] = jnp.where(r >= c, 1.0, 0.0).astype(BF16)

    @pl.when(i % per_b == 0)
    def _():
        carry_ref[...] = jnp.zeros(carry_ref.shape, F32)

    h = (x_ref[...] * (1.0 + sc_ref[0]) + sh_ref[0]).astype(BF16)
    z = jnp.dot(h, wf_ref[...], preferred_element_type=F32) + bf_ref[...]
    lf = jnp.minimum(z, 0.0) - jnp.log(1.0 + jnp.exp(-jnp.abs(z)))
    p1 = lf.astype(BF16)
    r1 = lf - p1.astype(F32)
    p2 = r1.astype(BF16)
    p3 = (r1 - p2.astype(F32)).astype(BF16)
    tri = tri_ref[...]
    cs = (jnp.dot(tri, p1, preferred_element_type=F32) + jnp.dot(tri, p2, preferred_element_type=F32)
          + jnp.dot(tri, p3, preferred_element_type=F32)) + carry_ref[...]
    carry_ref[...] = cs[tm - 1:tm, :]
    lane = lax.broadcasted_iota(I32, cs.shape, 1)
    gl = jnp.where(lane < N_HEADS, cs * LOG2E, 0.0)
    g_hi = gl.astype(BF16).astype(F32)
    g_r = gl - g_hi
    g_mid = g_r.astype(BF16).astype(F32)
    g_lo = (g_r - g_mid).astype(BF16).astype(F32)
    gcat = (g_hi + pltpu.roll(g_mid, N_HEADS, 1) + pltpu.roll(g_lo, 2 * N_HEADS, 1)).astype(BF16)
    g_ref[...] = gcat
    k_ref[...] = (jnp.dot(h, wk_ref[...], preferred_element_type=F32)
                  + jnp.dot(gcat, pk_ref[...], preferred_element_type=F32) + ck_ref[...]).astype(BF16)
    vt = lax.dot_general(wv_ref[...], h, (((1,), (1,)), ((), ())), preferred_element_type=F32)
    v_ref[0] = (vt + cv_ref[...]).astype(BF16)


def _kvf_call(x, sc, sh, wk, wv, wf, bf, pk, ck, cv, seq, tm=512):
    N, D = x.shape
    HW = wk.shape[1]
    per_b = seq // tm
    const = lambda i: (0, 0)
    bat = lambda i: (i // per_b, 0, 0)
    row = lambda i: (i, 0)
    return pl.pallas_call(
        functools.partial(_kvf_kernel, per_b=per_b),
        grid=(N // tm,),
        in_specs=[
            pl.BlockSpec((tm, D), row),
            pl.BlockSpec((1, 1, D), bat),
            pl.BlockSpec((1, 1, D), bat),
            pl.BlockSpec((D, HW), const),
            pl.BlockSpec((HW, D), const),
            pl.BlockSpec((D, LANES), const),
            pl.BlockSpec((1, LANES), const),
            pl.BlockSpec((LANES, HW), const),
            pl.BlockSpec((1, HW), const),
            pl.BlockSpec((HW, 1), const),
        ],
        out_specs=[
            pl.BlockSpec((tm, HW), row),
            pl.BlockSpec((1, HW, tm), lambda i: (i // per_b, 0, i % per_b)),
            pl.BlockSpec((tm, LANES), row),
        ],
        out_shape=[
            jax.ShapeDtypeStruct((N, HW), BF16),
            jax.ShapeDtypeStruct((N // seq, HW, seq), BF16),
            jax.ShapeDtypeStruct((N, LANES), BF16),
        ],
        scratch_shapes=[pltpu.VMEM((tm, tm), BF16), pltpu.VMEM((1, LANES), F32)],
        compiler_params=_cparams(("arbitrary",)),
        name="shared_kv_forget",
    )(x, sc, sh, wk, wv, wf, bf, pk, ck, cv)


def _qproj_kernel(x_ref, sc_ref, sh_ref, w_ref, g_ref, pq_ref, cq_ref, o_ref, *, qscale):
    h = (x_ref[...] * (1.0 + sc_ref[0]) + sh_ref[0]).astype(BF16)
    q = jnp.dot(h, w_ref[...], preferred_element_type=F32) * qscale
    o_ref[...] = (q + jnp.dot(g_ref[...], pq_ref[...], preferred_element_type=F32) + cq_ref[...]).astype(BF16)


def _qproj_call(x, sc, sh, w, gcat, pq, cq, seq, tm=512):
    N, D = x.shape
    HW = w.shape[1]
    per_b = seq // tm
    const = lambda i: (0, 0)
    bat = lambda i: (i // per_b, 0, 0)
    row = lambda i: (i, 0)
    return pl.pallas_call(
        functools.partial(_qproj_kernel, qscale=HEAD_DIM ** -0.5 * LOG2E),
        grid=(N // tm,),
        in_specs=[
            pl.BlockSpec((tm, D), row),
            pl.BlockSpec((1, 1, D), bat),
            pl.BlockSpec((1, 1, D), bat),
            pl.BlockSpec((D, HW), const),
            pl.BlockSpec((tm, LANES), row),
            pl.BlockSpec((LANES, HW), const),
            pl.BlockSpec((1, HW), const),
        ],
        out_specs=pl.BlockSpec((tm, HW), row),
        out_shape=jax.ShapeDtypeStruct((N, HW), BF16),
        compiler_params=_cparams(("parallel",)),
        name="q_proj",
    )(x, sc, sh, w, gcat, pq, cq)


def _attn_kernel(q_ref, k_ref, vt_ref, o_ref, m_ref, acc_ref, *, tk, nh):
    qi = pl.program_id(2)
    tq = q_ref.shape[0]
    m_ref[...] = jnp.full(m_ref.shape, NEG_BIG, F32)
    acc_ref[...] = jnp.zeros(acc_ref.shape, F32)

    def scores(j, c):
        k0 = pl.multiple_of(j * tk, tk)
        hs = slice(c * HEAD_W, (c + 1) * HEAD_W)
        return lax.dot_general(k_ref[pl.ds(k0, tk), hs], q_ref[:, hs], (((1,), (1,)), ((), ())),
                               preferred_element_type=F32)

    def update(j, c, s, masked):
        k0 = pl.multiple_of(j * tk, tk)
        if masked:
            key = k0 + lax.broadcasted_iota(I32, (tk, tq), 0)
            qry = qi * tq + lax.broadcasted_iota(I32, (tk, tq), 1)
            s = jnp.where(qry >= key, s, NEG_BIG)
        m_old = m_ref[c]
        m_new = jnp.maximum(m_old, jnp.max(s, axis=0, keepdims=True))
        p = jnp.exp2(s - m_new)
        a = jnp.exp2(m_old - m_new)
        pv = jnp.dot(vt_ref[0, c * HEAD_W:(c + 1) * HEAD_W, pl.ds(k0, tk)], p.astype(BF16),
                     preferred_element_type=F32)
        acc_ref[c] = a * acc_ref[c] + pv
        m_ref[c] = m_new

    def tile(j, masked):
        ss = [scores(j, c) for c in range(nh)]
        for c in range(nh):
            update(j, c, ss[c], masked)

    n_full = qi * (tq // tk)

    def body(j, carry):
        tile(j, False)
        return carry

    lax.fori_loop(0, n_full, body, 0)
    for d in range(tq // tk):
        tile(n_full + d, True)
    outs = []
    for c in range(nh):
        a = acc_ref[c]
        if c % 2 == 0:
            outs.append(a[0:HEAD_DIM, :] / a[AUG:AUG + 1, :])
        else:
            outs.append(a[HEAD_DIM:, :] / a[0:1, :])
    o_ref[...] = jnp.concatenate(outs, axis=0).T.astype(o_ref.dtype)


def _attn_call(q, k, vt, batch, seq, tq=512, tk=512, nh=ATTN_HEADS):
    N, HW = q.shape
    wn = nh * HEAD_W
    ng = HW // wn
    nq = seq // tq
    return pl.pallas_call(
        functools.partial(_attn_kernel, tk=tk, nh=nh),
        grid=(batch, ng, nq),
        in_specs=[
            pl.BlockSpec((tq, wn), lambda b, h, i: (b * nq + i, h)),
            pl.BlockSpec((seq, wn), lambda b, h, i: (b, h)),
            pl.BlockSpec((1, wn, seq), lambda b, h, i: (b, h, 0)),
        ],
        out_specs=pl.BlockSpec((tq, nh * HEAD_DIM), lambda b, h, i: (b * nq + i, h)),
        out_shape=jax.ShapeDtypeStruct((N, ng * nh * HEAD_DIM), BF16),
        scratch_shapes=[
            pltpu.VMEM((nh, 1, tq), F32),
            pltpu.VMEM((nh, HEAD_W, tq), F32),
        ],
        compiler_params=_cparams(("parallel", "parallel", "arbitrary")),
        name="fox_attention",
    )(q, k, vt)


def kernel(x, c, w_ada, b_ada, ln_mix_g, ln_mix_b, ln_ffn_g, ln_ffn_b, w_pw1, b_pw1, w_dw, b_dw, conv_ln_g,
           conv_ln_b, w_pw2, b_pw2, w_ada_kv, b_ada_kv, w_kvf, b_f, w_q, w_o, router_w, router_b, w_gate, w_up,
           w_down):
    B, S, D = x.shape
    N = B * S
    xf = x.reshape(N, D)

    mod = _mod_call(c, w_ada, b_ada[:, None, :])
    mod_kv = _mod_call(c, w_ada_kv[None], b_ada_kv[None, None, :])

    def mvec(l, idx):
        return mod[l, :, idx * D:(idx + 1) * D][:, None, :]

    rwt = router_w.T.astype(BF16)
    rb = router_b[:, None].astype(F32)
    zero_b = jnp.zeros((1, D), F32)
    place_k, place_q, const_k, const_q, const_v = _aug_constants()
    k = v = gcat = None

    for l in range(DEPTH):
        sh1, sc1, g1, sh2, sc2, g2 = (mvec(l, t) for t in range(6))
        if l < N_A_LAYERS:
            u = _glu_call(xf, sc1, sh1, w_pw1[l].astype(BF16), b_pw1[l][None, :], S)
            mix = _conv_call(u, w_dw[l][:, 0, :], b_dw[l][None, :], conv_ln_g[l][None, :], conv_ln_b[l][None, :],
                             B, S)
            w_out, b_out = w_pw2[l].astype(BF16), b_pw2[l][None, :]
        else:
            jl = l - N_A_LAYERS
            if l == N_A_LAYERS:
                wf = jnp.zeros((D, LANES), F32).at[:, :N_HEADS].set(w_kvf[:, 2 * D:]).astype(BF16)
                bf = jnp.zeros((1, LANES), F32).at[0, :N_HEADS].set(b_f)
                k, v, gcat = _kvf_call(xf, mod_kv[0, :, D:][:, None, :], mod_kv[0, :, :D][:, None, :],
                                       _spread_heads(w_kvf[:, :D], False).astype(BF16),
                                       _spread_heads(w_kvf[:, D:2 * D], True).T.astype(BF16),
                                       wf, bf, place_k, const_k, const_v, S)
            q = _qproj_call(xf, sc1, sh1, _spread_heads(w_q[jl], False).astype(BF16), gcat, place_q, const_q, S)
            mix = _attn_call(q, k, v, B, S)
            w_out, b_out = w_o[jl].astype(BF16), zero_b
        x1, h3, eid, gw = _post_call(mix, xf, w_out, b_out, g1, ln_mix_g[l][None, :], ln_mix_b[l][None, :],
                                     sc2, sh2, rwt, rb, S)
        bexp, idx = _route_meta(eid, N)
        ytok = _moe_call(bexp, idx, h3, w_gate[l].astype(BF16), w_up[l].astype(BF16), w_down[l].astype(BF16), N)
        xf = _combine_call(x1, ytok, gw, g2, ln_ffn_g[l][None, :], ln_ffn_b[l][None, :], S)
    return xf.reshape(B, S, D)
```

```python
import functools
import math

import numpy as np
import jax
import jax.numpy as jnp
from jax import lax
from jax.experimental import pallas as pl
from jax.experimental.pallas import tpu as pltpu

F32 = jnp.float32
BF16 = jnp.bfloat16
I32 = jnp.int32

DEPTH = 4
N_A_LAYERS = 2
N_HEADS = 16
HEAD_DIM = 64
CONV_K = 31
N_EXPERTS = 16
N_GROUPS = 4
EXPERTS_PER_GROUP = 4
ALPHA = (2.0 * DEPTH) ** 0.25
LN_EPS = 1e-5
LOG2E = math.log2(math.e)

LANES = 128
SUBLANES = 8
CONV_HALO = 32
MOE_ROWS = 512
MOE_SEGMENTS = 8
DMA_BATCH = 8
IDX_CHUNK = 1024
IDX_DST = 512
HEAD_W = LANES
AUG = HEAD_DIM
ATTN_HEADS = 4
NEG_BIG = -1e30
VMEM_LIMIT = 56 * 1024 * 1024


def _cparams(sem, vmem=VMEM_LIMIT):
    return pltpu.CompilerParams(dimension_semantics=sem, vmem_limit_bytes=vmem)


def _sigmoid(x):
    return 1.0 / (1.0 + jnp.exp(-x))


def _layer_norm(z, g, b):
    mu = jnp.mean(z, axis=-1, keepdims=True)
    zc = z - mu
    var = jnp.mean(zc * zc, axis=-1, keepdims=True)
    return zc * lax.rsqrt(var + LN_EPS) * g + b


def _shifted_out_zero(v):
    bits = lax.bitcast_convert_type(v, jnp.uint32)
    return lax.bitcast_convert_type(lax.shift_right_logical(bits, jnp.uint32(32)), F32)


def _rows_to_tiles(ref, val):
    t = val.shape[0]
    for s in range(val.shape[1] // LANES):
        ref[pl.ds(s, t, stride=SUBLANES), :] = val[:, s * LANES:(s + 1) * LANES]


def _tiles_to_rows(ref, t):
    return jnp.concatenate([ref[pl.ds(s, t, stride=SUBLANES), :] for s in range(SUBLANES)], axis=1)


def _mod_kernel(c_ref, w_ref, b_ref, o_ref):
    c = c_ref[...]
    ca = (c * _sigmoid(c)).astype(BF16)
    o_ref[0] = jnp.dot(ca, w_ref[0].astype(BF16), preferred_element_type=F32) + b_ref[0]


def _mod_call(c, w, b, tn=1024):
    L, D, M = w.shape
    B = c.shape[0]
    return pl.pallas_call(
        _mod_kernel,
        grid=(L, M // tn),
        in_specs=[
            pl.BlockSpec((B, D), lambda l, j: (0, 0)),
            pl.BlockSpec((1, D, tn), lambda l, j: (l, 0, j)),
            pl.BlockSpec((1, 1, tn), lambda l, j: (l, 0, j)),
        ],
        out_specs=pl.BlockSpec((1, B, tn), lambda l, j: (l, 0, j)),
        out_shape=jax.ShapeDtypeStruct((L, B, M), F32),
        compiler_params=_cparams(("arbitrary", "arbitrary")),
        name="adaln_mod",
    )(c, w, b)


def _glu_kernel(x_ref, sc_ref, sh_ref, w_ref, b_ref, o_ref):
    h = (x_ref[...] * (1.0 + sc_ref[0]) + sh_ref[0]).astype(BF16)
    d_out = o_ref.shape[1]
    a = jnp.dot(h, w_ref[:, :d_out], preferred_element_type=F32) + b_ref[:, :d_out]
    g = jnp.dot(h, w_ref[:, d_out:], preferred_element_type=F32) + b_ref[:, d_out:]
    o_ref[...] = (a * _sigmoid(g)).astype(o_ref.dtype)


def _glu_call(x, sc, sh, w, b, seq, tm=512):
    N, D = x.shape
    M = w.shape[1]
    per_b = seq // tm
    return pl.pallas_call(
        _glu_kernel,
        grid=(N // tm,),
        in_specs=[
            pl.BlockSpec((tm, D), lambda i: (i, 0)),
            pl.BlockSpec((1, 1, D), lambda i: (i // per_b, 0, 0)),
            pl.BlockSpec((1, 1, D), lambda i: (i // per_b, 0, 0)),
            pl.BlockSpec((D, M), lambda i: (0, 0)),
            pl.BlockSpec((1, M), lambda i: (0, 0)),
        ],
        out_specs=pl.BlockSpec((tm, M // 2), lambda i: (i, 0)),
        out_shape=jax.ShapeDtypeStruct((N, M // 2), BF16),
        compiler_params=_cparams(("parallel",)),
        name="pw1_glu",
    )(x, sc, sh, w, b)


def _conv_kernel(ucur_ref, uprev_ref, wdw_ref, bdw_ref, lg_ref, lb_ref, o_ref, ubuf, ybuf):
    j = pl.program_id(1)
    ts, D = ucur_ref.shape
    keep = jnp.where(j > 0, 1.0, 0.0)
    ubuf[0:CONV_HALO, :] = uprev_ref[...].astype(F32) * keep
    ubuf[CONV_HALO:CONV_HALO + ts, :] = ucur_ref[...].astype(F32)
    R, C = 64, LANES
    off = CONV_HALO - (CONV_K - 1)
    for rc in range(ts // R):
        base = rc * R
        for cc in range(D // C):
            cs = slice(cc * C, (cc + 1) * C)
            y = None
            for b in range(SUBLANES):
                rows = R if b == 0 else R + SUBLANES
                z = None
                for a in range((off + CONV_K - 1) // SUBLANES + 1):
                    o = SUBLANES * a + b
                    if o < off or o >= off + CONV_K:
                        continue
                    k = o - off
                    r0 = base + SUBLANES * a
                    term = ubuf[r0:r0 + rows, cs] * wdw_ref[k:k + 1, cs]
                    z = term if z is None else z + term
                zs = z[b:b + R]
                y = zs if y is None else y + zs
            ybuf[base:base + R, cs] = y + bdw_ref[:, cs]
    y = _layer_norm(ybuf[...], lg_ref[...], lb_ref[...])
    o_ref[...] = (y * _sigmoid(y)).astype(o_ref.dtype)


def _conv_call(u, wdw, bdw, lg, lb, batch, seq, ts=256):
    N, D = u.shape
    per_b = seq // ts
    hb = ts // CONV_HALO
    return pl.pallas_call(
        _conv_kernel,
        grid=(batch, per_b),
        in_specs=[
            pl.BlockSpec((ts, D), lambda b, j: (b * per_b + j, 0)),
            pl.BlockSpec((CONV_HALO, D), lambda b, j: (jnp.maximum((b * per_b + j) * hb - 1, 0), 0)),
            pl.BlockSpec((CONV_K, D), lambda b, j: (0, 0)),
            pl.BlockSpec((1, D), lambda b, j: (0, 0)),
            pl.BlockSpec((1, D), lambda b, j: (0, 0)),
            pl.BlockSpec((1, D), lambda b, j: (0, 0)),
        ],
        out_specs=pl.BlockSpec((ts, D), lambda b, j: (b * per_b + j, 0)),
        out_shape=jax.ShapeDtypeStruct((N, D), BF16),
        scratch_shapes=[pltpu.VMEM((CONV_HALO + ts, D), F32), pltpu.VMEM((ts, D), F32)],
        compiler_params=_cparams(("parallel", "parallel")),
        name="dwconv_ln_swish",
    )(u, u, wdw, bdw, lg, lb)


def _route_rows(lt):
    rows = [lt[e:e + 1, :] for e in range(N_EXPERTS)]
    m = rows[0]
    for r in rows[1:]:
        m = jnp.maximum(m, r)
    ex = [jnp.exp(r - m) for r in rows]
    tot = ex[0]
    for r in ex[1:]:
        tot = tot + r
    p = [r / tot for r in ex]
    scores = []
    for g in range(N_GROUPS):
        a, b, c, d = p[4 * g:4 * g + 4]
        hi1, lo1 = jnp.maximum(a, b), jnp.minimum(a, b)
        hi2, lo2 = jnp.maximum(c, d), jnp.minimum(c, d)
        top1 = jnp.maximum(hi1, hi2)
        top2 = jnp.maximum(jnp.minimum(hi1, hi2), jnp.maximum(lo1, lo2))
        scores.append(top1 + top2)
    best = scores[0]
    gi = jnp.zeros_like(best, dtype=I32)
    for g in range(1, N_GROUPS):
        upd = scores[g] > best
        best = jnp.where(upd, scores[g], best)
        gi = jnp.where(upd, g, gi)
    q = []
    for jx in range(EXPERTS_PER_GROUP):
        q.append(jnp.where(gi == 0, p[jx],
                           jnp.where(gi == 1, p[4 + jx], jnp.where(gi == 2, p[8 + jx], p[12 + jx]))))
    v1 = q[0]
    i1 = jnp.zeros_like(gi)
    for jx in range(1, EXPERTS_PER_GROUP):
        upd = q[jx] > v1
        v1 = jnp.where(upd, q[jx], v1)
        i1 = jnp.where(upd, jx, i1)
    qm = [jnp.where(i1 == jx, -1.0, q[jx]) for jx in range(EXPERTS_PER_GROUP)]
    v2 = qm[0]
    i2 = jnp.zeros_like(gi)
    for jx in range(1, EXPERTS_PER_GROUP):
        upd = qm[jx] > v2
        v2 = jnp.where(upd, qm[jx], v2)
        i2 = jnp.where(upd, jx, i2)
    den = v1 + v2
    return gi * EXPERTS_PER_GROUP + i1, gi * EXPERTS_PER_GROUP + i2, v1 / den, v2 / den


def _post_kernel(v_ref, x_ref, w_ref, b_ref, g1_ref, lg_ref, lb_ref, sc2_ref, sh2_ref, rwt_ref, rb_ref,
                 x1_ref, h_ref, eid_ref, gw_ref):
    y = jnp.dot(v_ref[...], w_ref[...], preferred_element_type=F32) + b_ref[...]
    z = ALPHA * x_ref[...] + (1.0 + g1_ref[0]) * y
    x1 = _layer_norm(z, lg_ref[...], lb_ref[...])
    x1_ref[...] = x1
    h = x1 * (1.0 + sc2_ref[0]) + sh2_ref[0]
    _rows_to_tiles(h_ref, h)
    lt = lax.dot_general(rwt_ref[...], h.astype(BF16), (((1,), (1,)), ((), ())),
                         preferred_element_type=F32) + rb_ref[...]
    e1, e2, w1, w2 = _route_rows(lt)
    eid_ref[0:1, :] = e1
    eid_ref[1:2, :] = e2
    gw_ref[0:1, :] = w1
    gw_ref[1:2, :] = w2


def _post_call(v, x, w, b, g1, lg, lb, sc2, sh2, rwt, rb, seq, tm=512):
    N, D = x.shape
    Dv = v.shape[1]
    E = rwt.shape[0]
    per_b = seq // tm
    row = lambda i: (i, 0)
    const = lambda i: (0, 0)
    bat = lambda i: (i // per_b, 0, 0)
    return pl.pallas_call(
        _post_kernel,
        grid=(N // tm,),
        in_specs=[
            pl.BlockSpec((tm, Dv), row),
            pl.BlockSpec((tm, D), row),
            pl.BlockSpec((Dv, D), const),
            pl.BlockSpec((1, D), const),
            pl.BlockSpec((1, 1, D), bat),
            pl.BlockSpec((1, D), const),
            pl.BlockSpec((1, D), const),
            pl.BlockSpec((1, 1, D), bat),
            pl.BlockSpec((1, 1, D), bat),
            pl.BlockSpec((E, D), const),
            pl.BlockSpec((E, 1), const),
        ],
        out_specs=[
            pl.BlockSpec((tm, D), row),
            pl.BlockSpec((tm * SUBLANES, LANES), row),
            pl.BlockSpec((2, tm), lambda i: (0, i)),
            pl.BlockSpec((2, tm), lambda i: (0, i)),
        ],
        out_shape=[
            jax.ShapeDtypeStruct((N, D), F32),
            jax.ShapeDtypeStruct((N * SUBLANES, LANES), F32),
            jax.ShapeDtypeStruct((2, N), I32),
            jax.ShapeDtypeStruct((2, N), F32),
        ],
        compiler_params=_cparams(("parallel",)),
        name="mixer_out_ln_route",
    )(v, x, w, b, g1, lg, lb, sc2, sh2, rwt, rb)


def _moe_kernel(bexp_ref, idx_hbm, h_hbm, wg_ref, wu_ref, wd_ref, ytok_hbm,
                idx_s, xbuf, ybuf, hm_ref, isem, gsem, ssem, *, nblk):
    del bexp_ref
    i = pl.program_id(0)
    slot = i % 2
    other = 1 - slot
    mb = hm_ref.shape[0]
    fd = hm_ref.shape[1]

    def tile(r):
        if isinstance(r, int):
            return pl.ds(r * SUBLANES, SUBLANES)
        return pl.ds(pl.multiple_of(r, SUBLANES), SUBLANES)

    def idx_copy(chunk):
        s4 = chunk % 4
        return pltpu.make_async_copy(idx_hbm.at[pl.ds(chunk * IDX_CHUNK, IDX_CHUNK)], idx_s.at[s4], isem.at[s4])

    def gather_start(chunk, xs, lo, hi):
        s4 = chunk % 4
        for j0 in range(lo, hi, DMA_BATCH):
            toks = [idx_s[s4, j] for j in range(j0, min(j0 + DMA_BATCH, hi))]
            for dj, t in enumerate(toks):
                pltpu.make_async_copy(h_hbm.at[tile(t)], xbuf.at[xs, tile(j0 + dj)], gsem.at[xs]).start()

    def scatter_start(chunk, ys, lo, hi):
        s4 = chunk % 4
        for j0 in range(lo, hi, DMA_BATCH):
            rows = [idx_s[s4, IDX_DST + j] for j in range(j0, min(j0 + DMA_BATCH, hi))]
            for dj, r in enumerate(rows):
                pltpu.make_async_copy(ybuf.at[ys, tile(j0 + dj)], ytok_hbm.at[tile(r)], ssem.at[ys]).start(priority=1)

    def gather_wait(xs):
        pltpu.make_async_copy(h_hbm.at[pl.ds(0, mb * SUBLANES)], xbuf.at[xs], gsem.at[xs]).wait()

    def scatter_wait(ys):
        pltpu.make_async_copy(ybuf.at[ys], ytok_hbm.at[pl.ds(0, mb * SUBLANES)], ssem.at[ys]).wait()

    @pl.when(i == 0)
    def _():
        ybuf[1] = jnp.zeros(ybuf.shape[1:], F32)
        for c in range(3):
            idx_copy(c).start()
        for c in range(3):
            idx_copy(c).wait()
        gather_start(1, 0, 0, mb)

    @pl.when(i > 0)
    def _():
        idx_copy(i + 2).wait()

    @pl.when(i + 3 <= nblk + 1)
    def _():
        idx_copy(i + 3).start()

    gather_wait(slot)

    @pl.when(i > 0)
    def _():
        scatter_wait(slot)

    half = MOE_SEGMENTS // 2
    per_seg = mb // half
    nc = fd // half
    xc = _tiles_to_rows(xbuf.at[slot], mb).astype(BF16)
    dep = None
    for sg in range(MOE_SEGMENTS):
        if sg < half:
            xd = xc if dep is None else xc + dep
            gather_start(i + 2, other, sg * per_seg, (sg + 1) * per_seg)
            scatter_start(i, other, sg * per_seg, (sg + 1) * per_seg)
            z = _shifted_out_zero(xbuf[slot, 0:SUBLANES, :])
            ybuf[slot, 0:SUBLANES, :] = z
            z = z + _shifted_out_zero(ybuf[slot, pl.ds(pl.multiple_of(slot * SUBLANES, SUBLANES), SUBLANES), :])
            dep = jnp.concatenate([z[0:1, :]] * (xc.shape[1] // LANES), axis=1).astype(BF16)
            cs = slice(sg * nc, (sg + 1) * nc)
            g = jnp.dot(xd, wg_ref[0, :, cs], preferred_element_type=F32)
            u = jnp.dot(xd, wu_ref[0, :, cs], preferred_element_type=F32)
            hm_ref[:, cs] = (g * _sigmoid(g) * u).astype(BF16)
        else:
            n = sg - half
            dn = wd_ref.shape[2] // half
            y = jnp.dot(hm_ref[...], wd_ref[0, :, n * dn:(n + 1) * dn], preferred_element_type=F32)
            for s in range(dn // LANES):
                ybuf[slot, pl.ds(n * (dn // LANES) + s, mb, stride=SUBLANES), :] = y[:, s * LANES:(s + 1) * LANES]

    @pl.when(i == nblk - 1)
    def _():
        scatter_start(i + 1, slot, 0, mb)
        gather_wait(other)
        scatter_wait(other)
        scatter_wait(slot)


def _moe_call(bexp, idx, h3, wg, wu, wd, n_tok):
    nblk = bexp.shape[0]
    E, D, Fd = wg.shape
    mb = MOE_ROWS
    wspec = lambda shape: pl.BlockSpec(shape, lambda i, be: (be[i], 0, 0))
    grid_spec = pltpu.PrefetchScalarGridSpec(
        num_scalar_prefetch=1,
        grid=(nblk,),
        in_specs=[
            pl.BlockSpec(memory_space=pl.ANY),
            pl.BlockSpec(memory_space=pl.ANY),
            wspec((1, D, Fd)),
            wspec((1, D, Fd)),
            wspec((1, Fd, D)),
        ],
        out_specs=pl.BlockSpec(memory_space=pl.ANY),
        scratch_shapes=[
            pltpu.SMEM((4, IDX_CHUNK), I32),
            pltpu.VMEM((2, mb * SUBLANES, LANES), F32),
            pltpu.VMEM((2, mb * SUBLANES, LANES), F32),
            pltpu.VMEM((mb, Fd), BF16),
            pltpu.SemaphoreType.DMA((4,)),
            pltpu.SemaphoreType.DMA((2,)),
            pltpu.SemaphoreType.DMA((2,)),
        ],
    )
    return pl.pallas_call(
        functools.partial(_moe_kernel, nblk=nblk),
        grid_spec=grid_spec,
        out_shape=jax.ShapeDtypeStruct(((2 * n_tok + 2 * mb) * SUBLANES, LANES), F32),
        compiler_params=_cparams(("arbitrary",)),
        name="moe_grouped_mlp",
    )(bexp, idx, h3, wg, wu, wd)


def _route_meta(eid, n_tok):
    mb = MOE_ROWS
    A = 2 * n_tok
    e = eid.reshape(-1)
    order = jnp.argsort(e).astype(I32)
    counts = jnp.bincount(e, length=N_EXPERTS).astype(I32)
    padded = (counts + mb - 1) // mb * mb
    pad_end = jnp.cumsum(padded)
    pad_start = pad_end - padded
    start = jnp.cumsum(counts) - counts
    nblk = A // mb + N_EXPERTS
    blk = jnp.arange(nblk, dtype=I32)
    bexp = jnp.minimum(jnp.searchsorted(pad_end, blk * mb, side="right"), N_EXPERTS - 1).astype(I32)
    j = jnp.arange(mb, dtype=I32)[None, :]
    r = blk[:, None] * mb + j - pad_start[bexp][:, None]
    valid = r < counts[bexp][:, None]
    a = order[jnp.clip(start[bexp][:, None] + r, 0, A - 1)]
    src = jnp.where(valid, jnp.where(a >= n_tok, a - n_tok, a), 0)
    dummy = A + (blk[:, None] % 2) * mb + j
    dst = jnp.where(valid, a, dummy)
    chunk = jnp.zeros((nblk + 2, IDX_CHUNK), I32)
    chunk = chunk.at[:, IDX_DST:IDX_DST + mb].set(jnp.broadcast_to(A + mb + j, (nblk + 2, mb)))
    chunk = chunk.at[1:nblk + 1, :mb].set(src)
    chunk = chunk.at[1:nblk + 1, IDX_DST:IDX_DST + mb].set(dst)
    return bexp, (chunk * SUBLANES).reshape(-1)


def _combine_kernel(x1_ref, y0_ref, y1_ref, gw_ref, g2_ref, lg_ref, lb_ref, o_ref):
    tm = x1_ref.shape[0]
    gw = gw_ref[...]
    eye = lax.broadcasted_iota(I32, (tm, tm), 0) == lax.broadcasted_iota(I32, (tm, tm), 1)
    w0 = jnp.sum(jnp.where(eye, jnp.broadcast_to(gw[0:1, :], (tm, tm)), 0.0), axis=1, keepdims=True)
    w1 = jnp.sum(jnp.where(eye, jnp.broadcast_to(gw[1:2, :], (tm, tm)), 0.0), axis=1, keepdims=True)
    y = w0 * _tiles_to_rows(y0_ref, tm) + w1 * _tiles_to_rows(y1_ref, tm)
    z = ALPHA * x1_ref[...] + (1.0 + g2_ref[0]) * y
    o_ref[...] = _layer_norm(z, lg_ref[...], lb_ref[...])


def _combine_call(x1, ytok, gw, g2, lg, lb, seq, tm=256):
    N, D = x1.shape
    per_b = seq // tm
    nt = N // tm
    const = lambda i: (0, 0)
    return pl.pallas_call(
        _combine_kernel,
        grid=(nt,),
        in_specs=[
            pl.BlockSpec((tm, D), lambda i: (i, 0)),
            pl.BlockSpec((tm * SUBLANES, LANES), lambda i: (i, 0)),
            pl.BlockSpec((tm * SUBLANES, LANES), lambda i: (nt + i, 0)),
            pl.BlockSpec((2, tm), lambda i: (0, i)),
            pl.BlockSpec((1, 1, D), lambda i: (i // per_b, 0, 0)),
            pl.BlockSpec((1, D), const),
            pl.BlockSpec((1, D), const),
        ],
        out_specs=pl.BlockSpec((tm, D), lambda i: (i, 0)),
        out_shape=jax.ShapeDtypeStruct((N, D), F32),
        compiler_params=_cparams(("parallel",)),
        name="moe_combine_ln",
    )(x1, ytok, ytok, gw, g2, lg, lb)


def _aug_constants():
    hw = N_HEADS * HEAD_W
    place_k = np.zeros((LANES, hw), np.float32)
    place_q = np.zeros((LANES, hw), np.float32)
    const_k = np.zeros((1, hw), np.float32)
    const_q = np.zeros((1, hw), np.float32)
    const_v = np.zeros((hw, 1), np.float32)
    for h in range(N_HEADS):
        for part in range(3):
            place_k[part * N_HEADS + h, h * HEAD_W + AUG + part] = -1.0
            place_q[part * N_HEADS + h, h * HEAD_W + AUG + 3 + part] = 1.0
            const_q[0, h * HEAD_W + AUG + part] = 1.0
            const_k[0, h * HEAD_W + AUG + 3 + part] = 1.0
        const_v[h * HEAD_W + (AUG if h % 2 == 0 else 0), 0] = 1.0
    return (jnp.asarray(place_k, BF16), jnp.asarray(place_q, BF16), jnp.asarray(const_k), jnp.asarray(const_q),
            jnp.asarray(const_v))


def _spread_heads(w, odd_high):
    D = w.shape[0]
    w4 = w.reshape(D, N_HEADS // 2, 2, HEAD_DIM)
    z = jnp.zeros_like(w4[:, :, 0])
    even = jnp.concatenate([w4[:, :, 0], z], axis=-1)
    odd = jnp.concatenate([z, w4[:, :, 1]], axis=-1) if odd_high else jnp.concatenate([w4[:, :, 1], z], axis=-1)
    return jnp.stack([even, odd], axis=2).reshape(D, N_HEADS * HEAD_W)


def _kvf_kernel(x_ref, sc_ref, sh_ref, wk_ref, wv_ref, wf_ref, bf_ref, pk_ref, ck_ref, cv_ref,
                k_ref, v_ref, g_ref, tri_ref, carry_ref, *, per_b):
    i = pl.program_id(0)
    tm = x_ref.shape[0]

    @pl.when(i == 0)
    def _():
        r = lax.broadcasted_iota(I32, (tm, tm), 0)
        c = lax.broadcasted_iota(I32, (tm, tm), 1)
        tri_ref[...] = jnp.where(r >= c, 1.0, 0.0).astype(BF16)

    @pl.when(i % per_b == 0)
    def _():
        carry_ref[...] = jnp.zeros(carry_ref.shape, F32)

    h = (x_ref[...] * (1.0 + sc_ref[0]) + sh_ref[0]).astype(BF16)
    z = jnp.dot(h, wf_ref[...], preferred_element_type=F32) + bf_ref[...]
    lf = jnp.minimum(z, 0.0) - jnp.log(1.0 + jnp.exp(-jnp.abs(z)))
    p1 = lf.astype(BF16)
    r1 = lf - p1.astype(F32)
    p2 = r1.astype(BF16)
    p3 = (r1 - p2.astype(F32)).astype(BF16)
    tri = tri_ref[...]
    cs = (jnp.dot(tri, p1, preferred_element_type=F32) + jnp.dot(tri, p2, preferred_element_type=F32)
          + jnp.dot(tri, p3, preferred_element_type=F32)) + carry_ref[...]
    carry_ref[...] = cs[tm - 1:tm, :]
    lane = lax.broadcasted_iota(I32, cs.shape, 1)
    gl = jnp.where(lane < N_HEADS, cs * LOG2E, 0.0)
    g_hi = gl.astype(BF16).astype(F32)
    g_r = gl - g_hi
    g_mid = g_r.astype(BF16).astype(F32)
    g_lo = (g_r - g_mid).astype(BF16).astype(F32)
    gcat = (g_hi + pltpu.roll(g_mid, N_HEADS, 1) + pltpu.roll(g_lo, 2 * N_HEADS, 1)).astype(BF16)
    g_ref[...] = gcat
    k_ref[...] = (jnp.dot(h, wk_ref[...], preferred_element_type=F32)
                  + jnp.dot(gcat, pk_ref[...], preferred_element_type=F32) + ck_ref[...]).astype(BF16)
    vt = lax.dot_general(wv_ref[...], h, (((1,), (1,)), ((), ())), preferred_element_type=F32)
    v_ref[0] = (vt + cv_ref[...]).astype(BF16)


def _kvf_call(x, sc, sh, wk, wv, wf, bf, pk, ck, cv, seq, tm=512):
    N, D = x.shape
    HW = wk.shape[1]
    per_b = seq // tm
    const = lambda i: (0, 0)
    bat = lambda i: (i // per_b, 0, 0)
    row = lambda i: (i, 0)
    return pl.pallas_call(
        functools.partial(_kvf_kernel, per_b=per_b),
        grid=(N // tm,),
        in_specs=[
            pl.BlockSpec((tm, D), row),
            pl.BlockSpec((1, 1, D), bat),
            pl.BlockSpec((1, 1, D), bat),
            pl.BlockSpec((D, HW), const),
            pl.BlockSpec((HW, D), const),
            pl.BlockSpec((D, LANES), const),
            pl.BlockSpec((1, LANES), const),
            pl.BlockSpec((LANES, HW), const),
            pl.BlockSpec((1, HW), const),
            pl.BlockSpec((HW, 1), const),
        ],
        out_specs=[
            pl.BlockSpec((tm, HW), row),
            pl.BlockSpec((1, HW, tm), lambda i: (i // per_b, 0, i % per_b)),
            pl.BlockSpec((tm, LANES), row),
        ],
        out_shape=[
            jax.ShapeDtypeStruct((N, HW), BF16),
            jax.ShapeDtypeStruct((N // seq, HW, seq), BF16),
            jax.ShapeDtypeStruct((N, LANES), BF16),
        ],
        scratch_shapes=[pltpu.VMEM((tm, tm), BF16), pltpu.VMEM((1, LANES), F32)],
        compiler_params=_cparams(("arbitrary",)),
        name="shared_kv_forget",
    )(x, sc, sh, wk, wv, wf, bf, pk, ck, cv)


def _qproj_kernel(x_ref, sc_ref, sh_ref, w_ref, g_ref, pq_ref, cq_ref, o_ref, *, qscale):
    h = (x_ref[...] * (1.0 + sc_ref[0]) + sh_ref[0]).astype(BF16)
    q = jnp.dot(h, w_ref[...], preferred_element_type=F32) * qscale
    o_ref[...] = (q + jnp.dot(g_ref[...], pq_ref[...], preferred_element_type=F32) + cq_ref[...]).astype(BF16)


def _qproj_call(x, sc, sh, w, gcat, pq, cq, seq, tm=512):
    N, D = x.shape
    HW = w.shape[1]
    per_b = seq // tm
    const = lambda i: (0, 0)
    bat = lambda i: (i // per_b, 0, 0)
    row = lambda i: (i, 0)
    return pl.pallas_call(
        functools.partial(_qproj_kernel, qscale=HEAD_DIM ** -0.5 * LOG2E),
        grid=(N // tm,),
        in_specs=[
            pl.BlockSpec((tm, D), row),
            pl.BlockSpec((1, 1, D), bat),
            pl.BlockSpec((1, 1, D), bat),
            pl.BlockSpec((D, HW), const),
            pl.BlockSpec((tm, LANES), row),
            pl.BlockSpec((LANES, HW), const),
            pl.BlockSpec((1, HW), const),
        ],
        out_specs=pl.BlockSpec((tm, HW), row),
        out_shape=jax.ShapeDtypeStruct((N, HW), BF16),
        compiler_params=_cparams(("parallel",)),
        name="q_proj",
    )(x, sc, sh, w, gcat, pq, cq)


def _attn_kernel(q_ref, k_ref, vt_ref, o_ref, m_ref, acc_ref, *, tk, nh):
    qi = pl.program_id(2)
    tq = q_ref.shape[0]
    m_ref[...] = jnp.full(m_ref.shape, NEG_BIG, F32)
    acc_ref[...] = jnp.zeros(acc_ref.shape, F32)

    def scores(j, c):
        k0 = pl.multiple_of(j * tk, tk)
        hs = slice(c * HEAD_W, (c + 1) * HEAD_W)
        return lax.dot_general(k_ref[pl.ds(k0, tk), hs], q_ref[:, hs], (((1,), (1,)), ((), ())),
                               preferred_element_type=F32)

    def update(j, c, s, masked):
        k0 = pl.multiple_of(j * tk, tk)
        if masked:
            key = k0 + lax.broadcasted_iota(I32, (tk, tq), 0)
            qry = qi * tq + lax.broadcasted_iota(I32, (tk, tq), 1)
            s = jnp.where(qry >= key, s, NEG_BIG)
        m_old = m_ref[c]
        m_new = jnp.maximum(m_old, jnp.max(s, axis=0, keepdims=True))
        p = jnp.exp2(s - m_new)
        a = jnp.exp2(m_old - m_new)
        pv = jnp.dot(vt_ref[0, c * HEAD_W:(c + 1) * HEAD_W, pl.ds(k0, tk)], p.astype(BF16),
                     preferred_element_type=F32)
        acc_ref[c] = a * acc_ref[c] + pv
        m_ref[c] = m_new

    def tile(j, masked):
        ss = [scores(j, c) for c in range(nh)]
        for c in range(nh):
            update(j, c, ss[c], masked)

    n_full = qi * (tq // tk)

    def body(j, carry):
        tile(j, False)
        return carry

    lax.fori_loop(0, n_full, body, 0)
    for d in range(tq // tk):
        tile(n_full + d, True)
    outs = []
    for c in range(nh):
        a = acc_ref[c]
        if c % 2 == 0:
            outs.append(a[0:HEAD_DIM, :] / a[AUG:AUG + 1, :])
        else:
            outs.append(a[HEAD_DIM:, :] / a[0:1, :])
    o_ref[...] = jnp.concatenate(outs, axis=0).T.astype(o_ref.dtype)


def _attn_call(q, k, vt, batch, seq, tq=512, tk=512, nh=ATTN_HEADS):
    N, HW = q.shape
    wn = nh * HEAD_W
    ng = HW // wn
    nq = seq // tq
    return pl.pallas_call(
        functools.partial(_attn_kernel, tk=tk, nh=nh),
        grid=(batch, ng, nq),
        in_specs=[
            pl.BlockSpec((tq, wn), lambda b, h, i: (b * nq + i, h)),
            pl.BlockSpec((seq, wn), lambda b, h, i: (b, h)),
            pl.BlockSpec((1, wn, seq), lambda b, h, i: (b, h, 0)),
        ],
        out_specs=pl.BlockSpec((tq, nh * HEAD_DIM), lambda b, h, i: (b * nq + i, h)),
        out_shape=jax.ShapeDtypeStruct((N, ng * nh * HEAD_DIM), BF16),
        scratch_shapes=[
            pltpu.VMEM((nh, 1, tq), F32),
            pltpu.VMEM((nh, HEAD_W, tq), F32),
        ],
        compiler_params=_cparams(("parallel", "parallel", "arbitrary")),
        name="fox_attention",
    )(q, k, vt)


def kernel(x, c, w_ada, b_ada, ln_mix_g, ln_mix_b, ln_ffn_g, ln_ffn_b, w_pw1, b_pw1, w_dw, b_dw, conv_ln_g,
           conv_ln_b, w_pw2, b_pw2, w_ada_kv, b_ada_kv, w_kvf, b_f, w_q, w_o, router_w, router_b, w_gate, w_up,
           w_down):
    B, S, D = x.shape
    N = B * S
    xf = x.reshape(N, D)

    mod = _mod_call(c, w_ada, b_ada[:, None, :])
    mod_kv = _mod_call(c, w_ada_kv[None], b_ada_kv[None, None, :])

    def mvec(l, idx):
        return mod[l, :, idx * D:(idx + 1) * D][:, None, :]

    rwt = router_w.T.astype(BF16)
    rb = router_b[:, None].astype(F32)
    zero_b = jnp.zeros((1, D), F32)
    place_k, place_q, const_k, const_q, const_v = _aug_constants()
    k = v = gcat = None

    for l in range(DEPTH):
        sh1, sc1, g1, sh2, sc2, g2 = (mvec(l, t) for t in range(6))
        if l < N_A_LAYERS:
            u = _glu_call(xf, sc1, sh1, w_pw1[l].astype(BF16), b_pw1[l][None, :], S)
            mix = _conv_call(u, w_dw[l][:, 0, :], b_dw[l][None, :], conv_ln_g[l][None, :], conv_ln_b[l][None, :],
                             B, S)
            w_out, b_out = w_pw2[l].astype(BF16), b_pw2[l][None, :]
        else:
            jl = l - N_A_LAYERS
            if l == N_A_LAYERS:
                wf = jnp.zeros((D, LANES), F32).at[:, :N_HEADS].set(w_kvf[:, 2 * D:]).astype(BF16)
                bf = jnp.zeros((1, LANES), F32).at[0, :N_HEADS].set(b_f)
                k, v, gcat = _kvf_call(xf, mod_kv[0, :, D:][:, None, :], mod_kv[0, :, :D][:, None, :],
                                       _spread_heads(w_kvf[:, :D], False).astype(BF16),
                                       _spread_heads(w_kvf[:, D:2 * D], True).T.astype(BF16),
                                       wf, bf, place_k, const_k, const_v, S)
            q = _qproj_call(xf, sc1, sh1, _spread_heads(w_q[jl], False).astype(BF16), gcat, place_q, const_q, S)
            mix = _attn_call(q, k, v, B, S)
            w_out, b_out = w_o[jl].astype(BF16), zero_b
        x1, h3, eid, gw = _post_call(mix, xf, w_out, b_out, g1, ln_mix_g[l][None, :], ln_mix_b[l][None, :],
                                     sc2, sh2, rwt, rb, S)
        bexp, idx = _route_meta(eid, N)
        ytok = _moe_call(bexp, idx, h3, w_gate[l].astype(BF16), w_up[l].astype(BF16), w_down[l].astype(BF16), N)
        xf = _combine_call(x1, ytok, gw, g2, ln_ffn_g[l][None, :], ln_ffn_b[l][None, :], S)
    return xf.reshape(B, S, D)
```

```python
import functools
import math

import numpy as np
import jax
import jax.numpy as jnp
from jax import lax
from jax.experimental import pallas as pl
from jax.experimental.pallas import tpu as pltpu

F32 = jnp.float32
BF16 = jnp.bfloat16
I32 = jnp.int32

DEPTH = 4
N_A_LAYERS = 2
N_HEADS = 16
HEAD_DIM = 64
CONV_K = 31
N_EXPERTS = 16
N_GROUPS = 4
EXPERTS_PER_GROUP = 4
ALPHA = (2.0 * DEPTH) ** 0.25
LN_EPS = 1e-5
LOG2E = math.log2(math.e)

LANES = 128
SUBLANES = 8
CONV_HALO = 32
MOE_ROWS = 512
MOE_SEGMENTS = 8
DMA_BATCH = 8
IDX_CHUNK = 1024
IDX_DST = 512
HEAD_W = LANES
AUG = HEAD_DIM
ATTN_HEADS = 4
ATTN_TILE = 512
NEG_BIG = -1e30
VMEM_LIMIT = 56 * 1024 * 1024


def _cparams(sem, vmem=VMEM_LIMIT):
    return pltpu.CompilerParams(dimension_semantics=sem, vmem_limit_bytes=vmem)


def _sigmoid(x):
    return 1.0 / (1.0 + jnp.exp(-x))


def _layer_norm(z, g, b):
    mu = jnp.mean(z, axis=-1, keepdims=True)
    zc = z - mu
    var = jnp.mean(zc * zc, axis=-1, keepdims=True)
    return zc * lax.rsqrt(var + LN_EPS) * g + b


def _shifted_out_zero(v):
    bits = lax.bitcast_convert_type(v, jnp.uint32)
    return lax.bitcast_convert_type(lax.shift_right_logical(bits, jnp.uint32(32)), F32)


def _rows_to_tiles(ref, val):
    t = val.shape[0]
    for s in range(val.shape[1] // LANES):
        ref[pl.ds(s, t, stride=SUBLANES), :] = val[:, s * LANES:(s + 1) * LANES]


def _tiles_to_rows(ref, t):
    return jnp.concatenate([ref[pl.ds(s, t, stride=SUBLANES), :] for s in range(SUBLANES)], axis=1)


def _mod_kernel(c_ref, w_ref, b_ref, o_ref):
    c = c_ref[...]
    ca = (c * _sigmoid(c)).astype(BF16)
    o_ref[0] = jnp.dot(ca, w_ref[0].astype(BF16), preferred_element_type=F32) + b_ref[0]


def _mod_call(c, w, b, tn=1024):
    L, D, M = w.shape
    B = c.shape[0]
    return pl.pallas_call(
        _mod_kernel,
        grid=(L, M // tn),
        in_specs=[
            pl.BlockSpec((B, D), lambda l, j: (0, 0)),
            pl.BlockSpec((1, D, tn), lambda l, j: (l, 0, j)),
            pl.BlockSpec((1, 1, tn), lambda l, j: (l, 0, j)),
        ],
        out_specs=pl.BlockSpec((1, B, tn), lambda l, j: (l, 0, j)),
        out_shape=jax.ShapeDtypeStruct((L, B, M), F32),
        compiler_params=_cparams(("arbitrary", "arbitrary")),
        name="adaln_mod",
    )(c, w, b)


def _glu_kernel(x_ref, sc_ref, sh_ref, w_ref, b_ref, o_ref):
    h = (x_ref[...] * (1.0 + sc_ref[0]) + sh_ref[0]).astype(BF16)
    d_out = o_ref.shape[1]
    a = jnp.dot(h, w_ref[:, :d_out], preferred_element_type=F32) + b_ref[:, :d_out]
    g = jnp.dot(h, w_ref[:, d_out:], preferred_element_type=F32) + b_ref[:, d_out:]
    o_ref[...] = (a * _sigmoid(g)).astype(o_ref.dtype)


def _glu_call(x, sc, sh, w, b, seq, tm=512):
    N, D = x.shape
    M = w.shape[1]
    per_b = seq // tm
    return pl.pallas_call(
        _glu_kernel,
        grid=(N // tm,),
        in_specs=[
            pl.BlockSpec((tm, D), lambda i: (i, 0)),
            pl.BlockSpec((1, 1, D), lambda i: (i // per_b, 0, 0)),
            pl.BlockSpec((1, 1, D), lambda i: (i // per_b, 0, 0)),
            pl.BlockSpec((D, M), lambda i: (0, 0)),
            pl.BlockSpec((1, M), lambda i: (0, 0)),
        ],
        out_specs=pl.BlockSpec((tm, M // 2), lambda i: (i, 0)),
        out_shape=jax.ShapeDtypeStruct((N, M // 2), BF16),
        compiler_params=_cparams(("parallel",)),
        name="pw1_glu",
    )(x, sc, sh, w, b)


def _conv_kernel(ucur_ref, uprev_ref, wdw_ref, bdw_ref, lg_ref, lb_ref, o_ref, ubuf, ybuf):
    j = pl.program_id(1)
    ts, D = ucur_ref.shape
    keep = jnp.where(j > 0, 1.0, 0.0)
    ubuf[0:CONV_HALO, :] = uprev_ref[...].astype(F32) * keep
    ubuf[CONV_HALO:CONV_HALO + ts, :] = ucur_ref[...].astype(F32)
    R, C = 64, LANES
    off = CONV_HALO - (CONV_K - 1)
    for rc in range(ts // R):
        base = rc * R
        for cc in range(D // C):
            cs = slice(cc * C, (cc + 1) * C)
            y = None
            for b in range(SUBLANES):
                rows = R if b == 0 else R + SUBLANES
                z = None
                for a in range((off + CONV_K - 1) // SUBLANES + 1):
                    o = SUBLANES * a + b
                    if o < off or o >= off + CONV_K:
                        continue
                    k = o - off
                    r0 = base + SUBLANES * a
                    term = ubuf[r0:r0 + rows, cs] * wdw_ref[k:k + 1, cs]
                    z = term if z is None else z + term
                zs = z[b:b + R]
                y = zs if y is None else y + zs
            ybuf[base:base + R, cs] = y + bdw_ref[:, cs]
    y = _layer_norm(ybuf[...], lg_ref[...], lb_ref[...])
    o_ref[...] = (y * _sigmoid(y)).astype(o_ref.dtype)


def _conv_call(u, wdw, bdw, lg, lb, batch, seq, ts=256):
    N, D = u.shape
    per_b = seq // ts
    hb = ts // CONV_HALO
    return pl.pallas_call(
        _conv_kernel,
        grid=(batch, per_b),
        in_specs=[
            pl.BlockSpec((ts, D), lambda b, j: (b * per_b + j, 0)),
            pl.BlockSpec((CONV_HALO, D), lambda b, j: (jnp.maximum((b * per_b + j) * hb - 1, 0), 0)),
            pl.BlockSpec((CONV_K, D), lambda b, j: (0, 0)),
            pl.BlockSpec((1, D), lambda b, j: (0, 0)),
            pl.BlockSpec((1, D), lambda b, j: (0, 0)),
            pl.BlockSpec((1, D), lambda b, j: (0, 0)),
        ],
        out_specs=pl.BlockSpec((ts, D), lambda b, j: (b * per_b + j, 0)),
        out_shape=jax.ShapeDtypeStruct((N, D), BF16),
        scratch_shapes=[pltpu.VMEM((CONV_HALO + ts, D), F32), pltpu.VMEM((ts, D), F32)],
        compiler_params=_cparams(("parallel", "parallel")),
        name="dwconv_ln_swish",
    )(u, u, wdw, bdw, lg, lb)


def _route_rows(lt):
    rows = [lt[e:e + 1, :] for e in range(N_EXPERTS)]
    m = rows[0]
    for r in rows[1:]:
        m = jnp.maximum(m, r)
    ex = [jnp.exp(r - m) for r in rows]
    tot = ex[0]
    for r in ex[1:]:
        tot = tot + r
    p = [r / tot for r in ex]
    scores = []
    for g in range(N_GROUPS):
        a, b, c, d = p[4 * g:4 * g + 4]
        hi1, lo1 = jnp.maximum(a, b), jnp.minimum(a, b)
        hi2, lo2 = jnp.maximum(c, d), jnp.minimum(c, d)
        top1 = jnp.maximum(hi1, hi2)
        top2 = jnp.maximum(jnp.minimum(hi1, hi2), jnp.maximum(lo1, lo2))
        scores.append(top1 + top2)
    best = scores[0]
    gi = jnp.zeros_like(best, dtype=I32)
    for g in range(1, N_GROUPS):
        upd = scores[g] > best
        best = jnp.where(upd, scores[g], best)
        gi = jnp.where(upd, g, gi)
    q = []
    for jx in range(EXPERTS_PER_GROUP):
        q.append(jnp.where(gi == 0, p[jx],
                           jnp.where(gi == 1, p[4 + jx], jnp.where(gi == 2, p[8 + jx], p[12 + jx]))))
    v1 = q[0]
    i1 = jnp.zeros_like(gi)
    for jx in range(1, EXPERTS_PER_GROUP):
        upd = q[jx] > v1
        v1 = jnp.where(upd, q[jx], v1)
        i1 = jnp.where(upd, jx, i1)
    qm = [jnp.where(i1 == jx, -1.0, q[jx]) for jx in range(EXPERTS_PER_GROUP)]
    v2 = qm[0]
    i2 = jnp.zeros_like(gi)
    for jx in range(1, EXPERTS_PER_GROUP):
        upd = qm[jx] > v2
        v2 = jnp.where(upd, qm[jx], v2)
        i2 = jnp.where(upd, jx, i2)
    den = v1 + v2
    return gi * EXPERTS_PER_GROUP + i1, gi * EXPERTS_PER_GROUP + i2, v1 / den, v2 / den


def _post_kernel(v_ref, x_ref, w_ref, b_ref, g1_ref, lg_ref, lb_ref, sc2_ref, sh2_ref, rwt_ref, rb_ref,
                 x1_ref, h_ref, eid_ref, gw_ref):
    y = jnp.dot(v_ref[...], w_ref[...], preferred_element_type=F32) + b_ref[...]
    z = ALPHA * x_ref[...] + (1.0 + g1_ref[0]) * y
    x1 = _layer_norm(z, lg_ref[...], lb_ref[...])
    x1_ref[...] = x1
    h = x1 * (1.0 + sc2_ref[0]) + sh2_ref[0]
    _rows_to_tiles(h_ref, h)
    lt = lax.dot_general(rwt_ref[...], h.astype(BF16), (((1,), (1,)), ((), ())),
                         preferred_element_type=F32) + rb_ref[...]
    e1, e2, w1, w2 = _route_rows(lt)
    eid_ref[0:1, :] = e1
    eid_ref[1:2, :] = e2
    gw_ref[0:1, :] = w1
    gw_ref[1:2, :] = w2


def _post_call(v, x, w, b, g1, lg, lb, sc2, sh2, rwt, rb, seq, tm=512):
    N, D = x.shape
    Dv = v.shape[1]
    E = rwt.shape[0]
    per_b = seq // tm
    row = lambda i: (i, 0)
    const = lambda i: (0, 0)
    bat = lambda i: (i // per_b, 0, 0)
    return pl.pallas_call(
        _post_kernel,
        grid=(N // tm,),
        in_specs=[
            pl.BlockSpec((tm, Dv), row),
            pl.BlockSpec((tm, D), row),
            pl.BlockSpec((Dv, D), const),
            pl.BlockSpec((1, D), const),
            pl.BlockSpec((1, 1, D), bat),
            pl.BlockSpec((1, D), const),
            pl.BlockSpec((1, D), const),
            pl.BlockSpec((1, 1, D), bat),
            pl.BlockSpec((1, 1, D), bat),
            pl.BlockSpec((E, D), const),
            pl.BlockSpec((E, 1), const),
        ],
        out_specs=[
            pl.BlockSpec((tm, D), row),
            pl.BlockSpec((tm * SUBLANES, LANES), row),
            pl.BlockSpec((2, tm), lambda i: (0, i)),
            pl.BlockSpec((2, tm), lambda i: (0, i)),
        ],
        out_shape=[
            jax.ShapeDtypeStruct((N, D), F32),
            jax.ShapeDtypeStruct((N * SUBLANES, LANES), F32),
            jax.ShapeDtypeStruct((2, N), I32),
            jax.ShapeDtypeStruct((2, N), F32),
        ],
        compiler_params=_cparams(("parallel",)),
        name="mixer_out_ln_route",
    )(v, x, w, b, g1, lg, lb, sc2, sh2, rwt, rb)


def _moe_kernel(bexp_ref, idx_hbm, h_hbm, wg_ref, wu_ref, wd_ref, ytok_hbm,
                idx_s, xbuf, ybuf, hm_ref, isem, gsem, ssem, *, nblk):
    del bexp_ref
    i = pl.program_id(0)
    slot = i % 2
    other = 1 - slot
    mb = hm_ref.shape[0]
    fd = hm_ref.shape[1]

    def tile(r):
        if isinstance(r, int):
            return pl.ds(r * SUBLANES, SUBLANES)
        return pl.ds(pl.multiple_of(r, SUBLANES), SUBLANES)

    def idx_copy(chunk):
        s4 = chunk % 4
        return pltpu.make_async_copy(idx_hbm.at[pl.ds(chunk * IDX_CHUNK, IDX_CHUNK)], idx_s.at[s4], isem.at[s4])

    def gather_start(chunk, xs, lo, hi):
        s4 = chunk % 4
        for j0 in range(lo, hi, DMA_BATCH):
            toks = [idx_s[s4, j] for j in range(j0, min(j0 + DMA_BATCH, hi))]
            for dj, t in enumerate(toks):
                pltpu.make_async_copy(h_hbm.at[tile(t)], xbuf.at[xs, tile(j0 + dj)], gsem.at[xs]).start()

    def scatter_start(chunk, ys, lo, hi):
        s4 = chunk % 4
        for j0 in range(lo, hi, DMA_BATCH):
            rows = [idx_s[s4, IDX_DST + j] for j in range(j0, min(j0 + DMA_BATCH, hi))]
            for dj, r in enumerate(rows):
                pltpu.make_async_copy(ybuf.at[ys, tile(j0 + dj)], ytok_hbm.at[tile(r)], ssem.at[ys]).start(priority=1)

    def gather_wait(xs):
        pltpu.make_async_copy(h_hbm.at[pl.ds(0, mb * SUBLANES)], xbuf.at[xs], gsem.at[xs]).wait()

    def scatter_wait(ys):
        pltpu.make_async_copy(ybuf.at[ys], ytok_hbm.at[pl.ds(0, mb * SUBLANES)], ssem.at[ys]).wait()

    @pl.when(i == 0)
    def _():
        ybuf[1] = jnp.zeros(ybuf.shape[1:], F32)
        for c in range(3):
            idx_copy(c).start()
        for c in range(3):
            idx_copy(c).wait()
        gather_start(1, 0, 0, mb)

    @pl.when(i > 0)
    def _():
        idx_copy(i + 2).wait()

    @pl.when(i + 3 <= nblk + 1)
    def _():
        idx_copy(i + 3).start()

    gather_wait(slot)

    @pl.when(i > 0)
    def _():
        scatter_wait(slot)

    half = MOE_SEGMENTS // 2
    per_seg = mb // half
    nc = fd // half
    xc = _tiles_to_rows(xbuf.at[slot], mb).astype(BF16)
    dep = None
    for sg in range(MOE_SEGMENTS):
        if sg < half:
            xd = xc if dep is None else xc + dep
            gather_start(i + 2, other, sg * per_seg, (sg + 1) * per_seg)
            scatter_start(i, other, sg * per_seg, (sg + 1) * per_seg)
            z = _shifted_out_zero(xbuf[slot, 0:SUBLANES, :])
            ybuf[slot, 0:SUBLANES, :] = z
            z = z + _shifted_out_zero(ybuf[slot, pl.ds(pl.multiple_of(slot * SUBLANES, SUBLANES), SUBLANES), :])
            dep = jnp.concatenate([z[0:1, :]] * (xc.shape[1] // LANES), axis=1).astype(BF16)
            cs = slice(sg * nc, (sg + 1) * nc)
            g = jnp.dot(xd, wg_ref[0, :, cs], preferred_element_type=F32)
            u = jnp.dot(xd, wu_ref[0, :, cs], preferred_element_type=F32)
            hm_ref[:, cs] = (g * _sigmoid(g) * u).astype(BF16)
        else:
            n = sg - half
            dn = wd_ref.shape[2] // half
            y = jnp.dot(hm_ref[...], wd_ref[0, :, n * dn:(n + 1) * dn], preferred_element_type=F32)
            for s in range(dn // LANES):
                ybuf[slot, pl.ds(n * (dn // LANES) + s, mb, stride=SUBLANES), :] = y[:, s * LANES:(s + 1) * LANES]

    @pl.when(i == nblk - 1)
    def _():
        scatter_start(i + 1, slot, 0, mb)
        gather_wait(other)
        scatter_wait(other)
        scatter_wait(slot)


def _moe_call(bexp, idx, h3, wg, wu, wd, n_tok, layer):
    nblk = bexp.shape[0]
    _, D, Fd = wg.shape
    mb = MOE_ROWS
    wspec = lambda shape: pl.BlockSpec(shape, lambda i, be: (layer * N_EXPERTS + be[i], 0, 0))
    grid_spec = pltpu.PrefetchScalarGridSpec(
        num_scalar_prefetch=1,
        grid=(nblk,),
        in_specs=[
            pl.BlockSpec(memory_space=pl.ANY),
            pl.BlockSpec(memory_space=pl.ANY),
            wspec((1, D, Fd)),
            wspec((1, D, Fd)),
            wspec((1, Fd, D)),
        ],
        out_specs=pl.BlockSpec(memory_space=pl.ANY),
        scratch_shapes=[
            pltpu.SMEM((4, IDX_CHUNK), I32),
            pltpu.VMEM((2, mb * SUBLANES, LANES), F32),
            pltpu.VMEM((2, mb * SUBLANES, LANES), F32),
            pltpu.VMEM((mb, Fd), BF16),
            pltpu.SemaphoreType.DMA((4,)),
            pltpu.SemaphoreType.DMA((2,)),
            pltpu.SemaphoreType.DMA((2,)),
        ],
    )
    return pl.pallas_call(
        functools.partial(_moe_kernel, nblk=nblk),
        grid_spec=grid_spec,
        out_shape=jax.ShapeDtypeStruct(((2 * n_tok + 2 * mb) * SUBLANES, LANES), F32),
        compiler_params=_cparams(("arbitrary",)),
        name="moe_grouped_mlp",
    )(bexp, idx, h3, wg, wu, wd)


def _route_meta(eid, n_tok):
    mb = MOE_ROWS
    A = 2 * n_tok
    e = eid.reshape(-1)
    order = jnp.argsort(e).astype(I32)
    counts = jnp.bincount(e, length=N_EXPERTS).astype(I32)
    padded = (counts + mb - 1) // mb * mb
    pad_end = jnp.cumsum(padded)
    pad_start = pad_end - padded
    start = jnp.cumsum(counts) - counts
    nblk = A // mb + N_EXPERTS
    blk = jnp.arange(nblk, dtype=I32)
    bexp = jnp.minimum(jnp.searchsorted(pad_end, blk * mb, side="right"), N_EXPERTS - 1).astype(I32)
    j = jnp.arange(mb, dtype=I32)[None, :]
    r = blk[:, None] * mb + j - pad_start[bexp][:, None]
    valid = r < counts[bexp][:, None]
    a = order[jnp.clip(start[bexp][:, None] + r, 0, A - 1)]
    src = jnp.where(valid, jnp.where(a >= n_tok, a - n_tok, a), 0)
    dummy = A + (blk[:, None] % 2) * mb + j
    dst = jnp.where(valid, a, dummy)
    chunk = jnp.zeros((nblk + 2, IDX_CHUNK), I32)
    chunk = chunk.at[:, IDX_DST:IDX_DST + mb].set(jnp.broadcast_to(A + mb + j, (nblk + 2, mb)))
    chunk = chunk.at[1:nblk + 1, :mb].set(src)
    chunk = chunk.at[1:nblk + 1, IDX_DST:IDX_DST + mb].set(dst)
    return bexp, (chunk * SUBLANES).reshape(-1)


def _combine_kernel(x1_ref, y0_ref, y1_ref, gw_ref, g2_ref, lg_ref, lb_ref, o_ref):
    tm = x1_ref.shape[0]
    gw = gw_ref[...]
    eye = lax.broadcasted_iota(I32, (tm, tm), 0) == lax.broadcasted_iota(I32, (tm, tm), 1)
    w0 = jnp.sum(jnp.where(eye, jnp.broadcast_to(gw[0:1, :], (tm, tm)), 0.0), axis=1, keepdims=True)
    w1 = jnp.sum(jnp.where(eye, jnp.broadcast_to(gw[1:2, :], (tm, tm)), 0.0), axis=1, keepdims=True)
    y = w0 * _tiles_to_rows(y0_ref, tm) + w1 * _tiles_to_rows(y1_ref, tm)
    z = ALPHA * x1_ref[...] + (1.0 + g2_ref[0]) * y
    o_ref[...] = _layer_norm(z, lg_ref[...], lb_ref[...])


def _combine_call(x1, ytok, gw, g2, lg, lb, seq, tm=256):
    N, D = x1.shape
    per_b = seq // tm
    nt = N // tm
    const = lambda i: (0, 0)
    return pl.pallas_call(
        _combine_kernel,
        grid=(nt,),
        in_specs=[
            pl.BlockSpec((tm, D), lambda i: (i, 0)),
            pl.BlockSpec((tm * SUBLANES, LANES), lambda i: (i, 0)),
            pl.BlockSpec((tm * SUBLANES, LANES), lambda i: (nt + i, 0)),
            pl.BlockSpec((2, tm), lambda i: (0, i)),
            pl.BlockSpec((1, 1, D), lambda i: (i // per_b, 0, 0)),
            pl.BlockSpec((1, D), const),
            pl.BlockSpec((1, D), const),
        ],
        out_specs=pl.BlockSpec((tm, D), lambda i: (i, 0)),
        out_shape=jax.ShapeDtypeStruct((N, D), F32),
        compiler_params=_cparams(("parallel",)),
        name="moe_combine_ln",
    )(x1, ytok, ytok, gw, g2, lg, lb)


def _aug_constants():
    hw = N_HEADS * HEAD_W
    place_k = np.zeros((LANES, hw), np.float32)
    place_q = np.zeros((LANES, hw), np.float32)
    const_k = np.zeros((1, hw), np.float32)
    const_q = np.zeros((1, hw), np.float32)
    const_v = np.zeros((hw, 1), np.float32)
    for h in range(N_HEADS):
        for part in range(3):
            place_k[part * N_HEADS + h, h * HEAD_W + AUG + part] = -1.0
            place_q[part * N_HEADS + h, h * HEAD_W + AUG + 3 + part] = 1.0
            const_q[0, h * HEAD_W + AUG + part] = 1.0
            const_k[0, h * HEAD_W + AUG + 3 + part] = 1.0
        const_v[h * HEAD_W + (AUG if h % 2 == 0 else 0), 0] = 1.0
    return (jnp.asarray(place_k, BF16), jnp.asarray(place_q, BF16), jnp.asarray(const_k), jnp.asarray(const_q),
            jnp.asarray(const_v))


def _spread_heads(w, odd_high):
    D = w.shape[0]
    w4 = w.reshape(D, N_HEADS // 2, 2, HEAD_DIM)
    z = jnp.zeros_like(w4[:, :, 0])
    even = jnp.concatenate([w4[:, :, 0], z], axis=-1)
    odd = jnp.concatenate([z, w4[:, :, 1]], axis=-1) if odd_high else jnp.concatenate([w4[:, :, 1], z], axis=-1)
    return jnp.stack([even, odd], axis=2).reshape(D, N_HEADS * HEAD_W)


def _kvf_kernel(x_ref, sc_ref, sh_ref, wk_ref, wv_ref, wf_ref, bf_ref, pk_ref, ck_ref, cv_ref,
                k_ref, v_ref, g_ref, tri_ref, carry_ref, *, per_b):
    i = pl.program_id(0)
    tm = x_ref.shape[0]

    @pl.when(i == 0)
    def _():
        r = lax.broadcasted_iota(I32, (tm, tm), 0)
        c = lax.broadcasted_iota(I32, (tm, tm), 1)
        tri_ref[...] = jnp.where(r >= c, 1.0, 0.0).astype(BF16)

    @pl.when(i % per_b == 0)
    def _():
        carry_ref[...] = jnp.zeros(carry_ref.shape, F32)

    h = (x_ref[...] * (1.0 + sc_ref[0]) + sh_ref[0]).astype(BF16)
    z = jnp.dot(h, wf_ref[...], preferred_element_type=F32) + bf_ref[...]
    lf = jnp.minimum(z, 0.0) - jnp.log(1.0 + jnp.exp(-jnp.abs(z)))
    p1 = lf.astype(BF16)
    r1 = lf - p1.astype(F32)
    p2 = r1.astype(BF16)
    p3 = (r1 - p2.astype(F32)).astype(BF16)
    tri = tri_ref[...]
    cs = (jnp.dot(tri, p1, preferred_element_type=F32) + jnp.dot(tri, p2, preferred_element_type=F32)
          + jnp.dot(tri, p3, preferred_element_type=F32)) + carry_ref[...]
    carry_ref[...] = cs[tm - 1:tm, :]
    lane = lax.broadcasted_iota(I32, cs.shape, 1)
    gl = jnp.where(lane < N_HEADS, cs * LOG2E, 0.0)
    g_hi = gl.astype(BF16).astype(F32)
    g_r = gl - g_hi
    g_mid = g_r.astype(BF16).astype(F32)
    g_lo = (g_r - g_mid).astype(BF16).astype(F32)
    gcat = (g_hi + pltpu.roll(g_mid, N_HEADS, 1) + pltpu.roll(g_lo, 2 * N_HEADS, 1)).astype(BF16)
    g_ref[...] = gcat
    k_ref[...] = (jnp.dot(h, wk_ref[...], preferred_element_type=F32)
                  + jnp.dot(gcat, pk_ref[...], preferred_element_type=F32) + ck_ref[...]).astype(BF16)
    vt = lax.dot_general(wv_ref[...], h, (((1,), (1,)), ((), ())), preferred_element_type=F32)
    v_ref[0] = (vt + cv_ref[...]).astype(BF16)


def _kvf_call(x, sc, sh, wk, wv, wf, bf, pk, ck, cv, seq, tm=512):
    N, D = x.shape
    HW = wk.shape[1]
    per_b = seq // tm
    const = lambda i: (0, 0)
    bat = lambda i: (i // per_b, 0, 0)
    row = lambda i: (i, 0)
    return pl.pallas_call(
        functools.partial(_kvf_kernel, per_b=per_b),
        grid=(N // tm,),
        in_specs=[
            pl.BlockSpec((tm, D), row),
            pl.BlockSpec((1, 1, D), bat),
            pl.BlockSpec((1, 1, D), bat),
            pl.BlockSpec((D, HW), const),
            pl.BlockSpec((HW, D), const),
            pl.BlockSpec((D, LANES), const),
            pl.BlockSpec((1, LANES), const),
            pl.BlockSpec((LANES, HW), const),
            pl.BlockSpec((1, HW), const),
            pl.BlockSpec((HW, 1), const),
        ],
        out_specs=[
            pl.BlockSpec((tm, HW), row),
            pl.BlockSpec((1, HW, tm), lambda i: (i // per_b, 0, i % per_b)),
            pl.BlockSpec((tm, LANES), row),
        ],
        out_shape=[
            jax.ShapeDtypeStruct((N, HW), BF16),
            jax.ShapeDtypeStruct((N // seq, HW, seq), BF16),
            jax.ShapeDtypeStruct((N, LANES), BF16),
        ],
        scratch_shapes=[pltpu.VMEM((tm, tm), BF16), pltpu.VMEM((1, LANES), F32)],
        compiler_params=_cparams(("arbitrary",)),
        name="shared_kv_forget",
    )(x, sc, sh, wk, wv, wf, bf, pk, ck, cv)


def _qproj_kernel(x_ref, sc_ref, sh_ref, w_ref, g_ref, pq_ref, cq_ref, o_ref, *, qscale):
    h = (x_ref[...] * (1.0 + sc_ref[0]) + sh_ref[0]).astype(BF16)
    q = jnp.dot(h, w_ref[...], preferred_element_type=F32) * qscale
    o_ref[...] = (q + jnp.dot(g_ref[...], pq_ref[...], preferred_element_type=F32) + cq_ref[...]).astype(BF16)


def _qproj_call(x, sc, sh, w, gcat, pq, cq, seq, tm=512):
    N, D = x.shape
    HW = w.shape[1]
    per_b = seq // tm
    const = lambda i: (0, 0)
    bat = lambda i: (i // per_b, 0, 0)
    row = lambda i: (i, 0)
    return pl.pallas_call(
        functools.partial(_qproj_kernel, qscale=HEAD_DIM ** -0.5 * LOG2E),
        grid=(N // tm,),
        in_specs=[
            pl.BlockSpec((tm, D), row),
            pl.BlockSpec((1, 1, D), bat),
            pl.BlockSpec((1, 1, D), bat),
            pl.BlockSpec((D, HW), const),
            pl.BlockSpec((tm, LANES), row),
            pl.BlockSpec((LANES, HW), const),
            pl.BlockSpec((1, HW), const),
        ],
        out_specs=pl.BlockSpec((tm, HW), row),
        out_shape=jax.ShapeDtypeStruct((N, HW), BF16),
        compiler_params=_cparams(("parallel",)),
        name="q_proj",
    )(x, sc, sh, w, gcat, pq, cq)


def _attn_kernel(q_ref, k_ref, vt_ref, o_ref, m_ref, acc_ref, sa_ref, sb_ref, *, tk, nh):
    qi = pl.program_id(2)
    tq = q_ref.shape[0]
    m_ref[...] = jnp.full(m_ref.shape, NEG_BIG, F32)
    acc_ref[...] = jnp.zeros(acc_ref.shape, F32)

    def scores(j, c):
        k0 = pl.multiple_of(j * tk, tk)
        hs = slice(c * HEAD_W, (c + 1) * HEAD_W)
        return lax.dot_general(k_ref[pl.ds(k0, tk), hs], q_ref[:, hs], (((1,), (1,)), ((), ())),
                               preferred_element_type=F32)

    def update(j, c, s, masked):
        k0 = pl.multiple_of(j * tk, tk)
        if masked:
            key = k0 + lax.broadcasted_iota(I32, (tk, tq), 0)
            qry = qi * tq + lax.broadcasted_iota(I32, (tk, tq), 1)
            s = jnp.where(qry >= key, s, NEG_BIG)
        m_old = m_ref[c]
        m_new = jnp.maximum(m_old, jnp.max(s, axis=0, keepdims=True))
        p = jnp.exp2(s - m_new)
        a = jnp.exp2(m_old - m_new)
        pv = jnp.dot(vt_ref[0, c * HEAD_W:(c + 1) * HEAD_W, pl.ds(k0, tk)], p.astype(BF16),
                     preferred_element_type=F32)
        acc_ref[c] = a * acc_ref[c] + pv
        m_ref[c] = m_new

    def stage(j, src, dst):
        for c in range(nh):
            dst[c] = scores(j + 1, c)
        for c in range(nh):
            update(j, c, src[c], False)

    for c in range(nh):
        sa_ref[c] = scores(0, c)

    def body(jj, carry):
        stage(2 * jj, sa_ref, sb_ref)
        stage(2 * jj + 1, sb_ref, sa_ref)
        return carry

    lax.fori_loop(0, qi // 2, body, 0)

    @pl.when(qi % 2 == 1)
    def _():
        stage(qi - 1, sa_ref, sb_ref)
        for c in range(nh):
            update(qi, c, sb_ref[c], True)

    @pl.when(qi % 2 == 0)
    def _():
        for c in range(nh):
            update(qi, c, sa_ref[c], True)

    outs = []
    for c in range(nh):
        a = acc_ref[c]
        if c % 2 == 0:
            outs.append(a[0:HEAD_DIM, :] / a[AUG:AUG + 1, :])
        else:
            outs.append(a[HEAD_DIM:, :] / a[0:1, :])
    o_ref[...] = jnp.concatenate(outs, axis=0).T.astype(o_ref.dtype)


def _attn_call(q, k, vt, batch, seq, tq=ATTN_TILE, nh=ATTN_HEADS):
    tk = tq
    N, HW = q.shape
    wn = nh * HEAD_W
    ng = HW // wn
    nq = seq // tq
    return pl.pallas_call(
        functools.partial(_attn_kernel, tk=tk, nh=nh),
        grid=(batch, ng, nq),
        in_specs=[
            pl.BlockSpec((tq, wn), lambda b, h, i: (b * nq + i, h)),
            pl.BlockSpec((seq, wn), lambda b, h, i: (b, h)),
            pl.BlockSpec((1, wn, seq), lambda b, h, i: (b, h, 0)),
        ],
        out_specs=pl.BlockSpec((tq, nh * HEAD_DIM), lambda b, h, i: (b * nq + i, h)),
        out_shape=jax.ShapeDtypeStruct((N, ng * nh * HEAD_DIM), BF16),
        scratch_shapes=[
            pltpu.VMEM((nh, 1, tq), F32),
            pltpu.VMEM((nh, HEAD_W, tq), F32),
            pltpu.VMEM((nh, tk, tq), F32),
            pltpu.VMEM((nh, tk, tq), F32),
        ],
        compiler_params=_cparams(("parallel", "parallel", "arbitrary")),
        name="fox_attention",
    )(q, k, vt)


def kernel(x, c, w_ada, b_ada, ln_mix_g, ln_mix_b, ln_ffn_g, ln_ffn_b, w_pw1, b_pw1, w_dw, b_dw, conv_ln_g,
           conv_ln_b, w_pw2, b_pw2, w_ada_kv, b_ada_kv, w_kvf, b_f, w_q, w_o, router_w, router_b, w_gate, w_up,
           w_down):
    B, S, D = x.shape
    N = B * S
    xf = x.reshape(N, D)

    mod = _mod_call(c, w_ada, b_ada[:, None, :])
    mod_kv = _mod_call(c, w_ada_kv[None], b_ada_kv[None, None, :])

    def mvec(l, idx):
        return mod[l, :, idx * D:(idx + 1) * D][:, None, :]

    rwt = router_w.T.astype(BF16)
    rb = router_b[:, None].astype(F32)
    zero_b = jnp.zeros((1, D), F32)
    place_k, place_q, const_k, const_q, const_v = _aug_constants()
    wg_all = w_gate.astype(BF16).reshape((DEPTH * N_EXPERTS,) + w_gate.shape[2:])
    wu_all = w_up.astype(BF16).reshape((DEPTH * N_EXPERTS,) + w_up.shape[2:])
    wd_all = w_down.astype(BF16).reshape((DEPTH * N_EXPERTS,) + w_down.shape[2:])
    k = v = gcat = None

    for l in range(DEPTH):
        sh1, sc1, g1, sh2, sc2, g2 = (mvec(l, t) for t in range(6))
        if l < N_A_LAYERS:
            u = _glu_call(xf, sc1, sh1, w_pw1[l].astype(BF16), b_pw1[l][None, :], S)
            mix = _conv_call(u, w_dw[l][:, 0, :], b_dw[l][None, :], conv_ln_g[l][None, :], conv_ln_b[l][None, :],
                             B, S)
            w_out, b_out = w_pw2[l].astype(BF16), b_pw2[l][None, :]
        else:
            jl = l - N_A_LAYERS
            if l == N_A_LAYERS:
                wf = jnp.zeros((D, LANES), F32).at[:, :N_HEADS].set(w_kvf[:, 2 * D:]).astype(BF16)
                bf = jnp.zeros((1, LANES), F32).at[0, :N_HEADS].set(b_f)
                k, v, gcat = _kvf_call(xf, mod_kv[0, :, D:][:, None, :], mod_kv[0, :, :D][:, None, :],
                                       _spread_heads(w_kvf[:, :D], False).astype(BF16),
                                       _spread_heads(w_kvf[:, D:2 * D], True).T.astype(BF16),
                                       wf, bf, place_k, const_k, const_v, S)
            q = _qproj_call(xf, sc1, sh1, _spread_heads(w_q[jl], False).astype(BF16), gcat, place_q, const_q, S)
            mix = _attn_call(q, k, v, B, S)
            w_out, b_out = w_o[jl].astype(BF16), zero_b
        x1, h3, eid, gw = _post_call(mix, xf, w_out, b_out, g1, ln_mix_g[l][None, :], ln_mix_b[l][None, :],
                                     sc2, sh2, rwt, rb, S)
        bexp, idx = _route_meta(eid, N)
        ytok = _moe_call(bexp, idx, h3, wg_all, wu_all, wd_all, N, l)
        xf = _combine_call(x1, ytok, gw, g2, ln_ffn_g[l][None, :], ln_ffn_b[l][None, :], S)
    return xf.reshape(B, S, D)
```

```python
import functools
import math

import numpy as np
import jax
import jax.numpy as jnp
from jax import lax
from jax.experimental import pallas as pl
from jax.experimental.pallas import tpu as pltpu

F32 = jnp.float32
BF16 = jnp.bfloat16
I32 = jnp.int32

DEPTH = 4
N_A_LAYERS = 2
N_HEADS = 16
HEAD_DIM = 64
CONV_K = 31
N_EXPERTS = 16
N_GROUPS = 4
EXPERTS_PER_GROUP = 4
ALPHA = (2.0 * DEPTH) ** 0.25
LN_EPS = 1e-5
LOG2E = math.log2(math.e)

LANES = 128
SUBLANES = 8
CONV_HALO = 32
MOE_ROWS = 512
MOE_SEGMENTS = 8
DMA_BATCH = 8
IDX_CHUNK = 1024
IDX_DST = 512
HEAD_W = LANES
AUG = HEAD_DIM
ATTN_HEADS = 4
ATTN_TILE = 512
NEG_BIG = -1e30
VMEM_LIMIT = 56 * 1024 * 1024


def _cparams(sem, vmem=VMEM_LIMIT):
    return pltpu.CompilerParams(dimension_semantics=sem, vmem_limit_bytes=vmem)


def _sigmoid(x):
    return 1.0 / (1.0 + jnp.exp(-x))


def _layer_norm(z, g, b):
    mu = jnp.mean(z, axis=-1, keepdims=True)
    zc = z - mu
    var = jnp.mean(zc * zc, axis=-1, keepdims=True)
    return zc * lax.rsqrt(var + LN_EPS) * g + b


def _shifted_out_zero(v):
    bits = lax.bitcast_convert_type(v, jnp.uint32)
    return lax.bitcast_convert_type(lax.shift_right_logical(bits, jnp.uint32(32)), F32)


def _rows_to_tiles(ref, val):
    t = val.shape[0]
    for s in range(val.shape[1] // LANES):
        ref[pl.ds(s, t, stride=SUBLANES), :] = val[:, s * LANES:(s + 1) * LANES]


def _tiles_to_rows(ref, t):
    return jnp.concatenate([ref[pl.ds(s, t, stride=SUBLANES), :] for s in range(SUBLANES)], axis=1)


def _mod_kernel(c_ref, w_ref, b_ref, o_ref):
    c = c_ref[...]
    ca = (c * _sigmoid(c)).astype(BF16)
    o_ref[0] = jnp.dot(ca, w_ref[0].astype(BF16), preferred_element_type=F32) + b_ref[0]


def _mod_call(c, w, b, tn=1024):
    L, D, M = w.shape
    B = c.shape[0]
    return pl.pallas_call(
        _mod_kernel,
        grid=(L, M // tn),
        in_specs=[
            pl.BlockSpec((B, D), lambda l, j: (0, 0)),
            pl.BlockSpec((1, D, tn), lambda l, j: (l, 0, j)),
            pl.BlockSpec((1, 1, tn), lambda l, j: (l, 0, j)),
        ],
        out_specs=pl.BlockSpec((1, B, tn), lambda l, j: (l, 0, j)),
        out_shape=jax.ShapeDtypeStruct((L, B, M), F32),
        compiler_params=_cparams(("arbitrary", "arbitrary")),
        name="adaln_mod",
    )(c, w, b)


def _glu_kernel(x_ref, sc_ref, sh_ref, w_ref, b_ref, o_ref):
    h = (x_ref[...] * (1.0 + sc_ref[0]) + sh_ref[0]).astype(BF16)
    d_out = o_ref.shape[1]
    a = jnp.dot(h, w_ref[:, :d_out], preferred_element_type=F32) + b_ref[:, :d_out]
    g = jnp.dot(h, w_ref[:, d_out:], preferred_element_type=F32) + b_ref[:, d_out:]
    o_ref[...] = (a * _sigmoid(g)).astype(o_ref.dtype)


def _glu_call(x, sc, sh, w, b, seq, tm=512):
    N, D = x.shape
    M = w.shape[1]
    per_b = seq // tm
    return pl.pallas_call(
        _glu_kernel,
        grid=(N // tm,),
        in_specs=[
            pl.BlockSpec((tm, D), lambda i: (i, 0)),
            pl.BlockSpec((1, 1, D), lambda i: (i // per_b, 0, 0)),
            pl.BlockSpec((1, 1, D), lambda i: (i // per_b, 0, 0)),
            pl.BlockSpec((D, M), lambda i: (0, 0)),
            pl.BlockSpec((1, M), lambda i: (0, 0)),
        ],
        out_specs=pl.BlockSpec((tm, M // 2), lambda i: (i, 0)),
        out_shape=jax.ShapeDtypeStruct((N, M // 2), BF16),
        compiler_params=_cparams(("parallel",)),
        name="pw1_glu",
    )(x, sc, sh, w, b)


def _conv_kernel(ucur_ref, uprev_ref, wdw_ref, bdw_ref, lg_ref, lb_ref, o_ref, ubuf, ybuf):
    j = pl.program_id(1)
    ts, D = ucur_ref.shape
    keep = jnp.where(j > 0, 1.0, 0.0)
    ubuf[0:CONV_HALO, :] = uprev_ref[...].astype(F32) * keep
    ubuf[CONV_HALO:CONV_HALO + ts, :] = ucur_ref[...].astype(F32)
    R, C = 64, LANES
    off = CONV_HALO - (CONV_K - 1)
    for rc in range(ts // R):
        base = rc * R
        for cc in range(D // C):
            cs = slice(cc * C, (cc + 1) * C)
            y = None
            for b in range(SUBLANES):
                rows = R if b == 0 else R + SUBLANES
                z = None
                for a in range((off + CONV_K - 1) // SUBLANES + 1):
                    o = SUBLANES * a + b
                    if o < off or o >= off + CONV_K:
                        continue
                    k = o - off
                    r0 = base + SUBLANES * a
                    term = ubuf[r0:r0 + rows, cs] * wdw_ref[k:k + 1, cs]
                    z = term if z is None else z + term
                zs = z[b:b + R]
                y = zs if y is None else y + zs
            ybuf[base:base + R, cs] = y + bdw_ref[:, cs]
    y = _layer_norm(ybuf[...], lg_ref[...], lb_ref[...])
    o_ref[...] = (y * _sigmoid(y)).astype(o_ref.dtype)


def _conv_call(u, wdw, bdw, lg, lb, batch, seq, ts=256):
    N, D = u.shape
    per_b = seq // ts
    hb = ts // CONV_HALO
    return pl.pallas_call(
        _conv_kernel,
        grid=(batch, per_b),
        in_specs=[
            pl.BlockSpec((ts, D), lambda b, j: (b * per_b + j, 0)),
            pl.BlockSpec((CONV_HALO, D), lambda b, j: (jnp.maximum((b * per_b + j) * hb - 1, 0), 0)),
            pl.BlockSpec((CONV_K, D), lambda b, j: (0, 0)),
            pl.BlockSpec((1, D), lambda b, j: (0, 0)),
            pl.BlockSpec((1, D), lambda b, j: (0, 0)),
            pl.BlockSpec((1, D), lambda b, j: (0, 0)),
        ],
        out_specs=pl.BlockSpec((ts, D), lambda b, j: (b * per_b + j, 0)),
        out_shape=jax.ShapeDtypeStruct((N, D), BF16),
        scratch_shapes=[pltpu.VMEM((CONV_HALO + ts, D), F32), pltpu.VMEM((ts, D), F32)],
        compiler_params=_cparams(("parallel", "parallel")),
        name="dwconv_ln_swish",
    )(u, u, wdw, bdw, lg, lb)


def _route_rows(lt):
    rows = [lt[e:e + 1, :] for e in range(N_EXPERTS)]
    m = rows[0]
    for r in rows[1:]:
        m = jnp.maximum(m, r)
    ex = [jnp.exp(r - m) for r in rows]
    tot = ex[0]
    for r in ex[1:]:
        tot = tot + r
    p = [r / tot for r in ex]
    scores = []
    for g in range(N_GROUPS):
        a, b, c, d = p[4 * g:4 * g + 4]
        hi1, lo1 = jnp.maximum(a, b), jnp.minimum(a, b)
        hi2, lo2 = jnp.maximum(c, d), jnp.minimum(c, d)
        top1 = jnp.maximum(hi1, hi2)
        top2 = jnp.maximum(jnp.minimum(hi1, hi2), jnp.maximum(lo1, lo2))
        scores.append(top1 + top2)
    best = scores[0]
    gi = jnp.zeros_like(best, dtype=I32)
    for g in range(1, N_GROUPS):
        upd = scores[g] > best
        best = jnp.where(upd, scores[g], best)
        gi = jnp.where(upd, g, gi)
    q = []
    for jx in range(EXPERTS_PER_GROUP):
        q.append(jnp.where(gi == 0, p[jx],
                           jnp.where(gi == 1, p[4 + jx], jnp.where(gi == 2, p[8 + jx], p[12 + jx]))))
    v1 = q[0]
    i1 = jnp.zeros_like(gi)
    for jx in range(1, EXPERTS_PER_GROUP):
        upd = q[jx] > v1
        v1 = jnp.where(upd, q[jx], v1)
        i1 = jnp.where(upd, jx, i1)
    qm = [jnp.where(i1 == jx, -1.0, q[jx]) for jx in range(EXPERTS_PER_GROUP)]
    v2 = qm[0]
    i2 = jnp.zeros_like(gi)
    for jx in range(1, EXPERTS_PER_GROUP):
        upd = qm[jx] > v2
        v2 = jnp.where(upd, qm[jx], v2)
        i2 = jnp.where(upd, jx, i2)
    den = v1 + v2
    return gi * EXPERTS_PER_GROUP + i1, gi * EXPERTS_PER_GROUP + i2, v1 / den, v2 / den


def _post_kernel(v_ref, x_ref, w_ref, b_ref, g1_ref, lg_ref, lb_ref, sc2_ref, sh2_ref, rwt_ref, rb_ref,
                 x1_ref, h_ref, eid_ref, gw_ref):
    y = jnp.dot(v_ref[...], w_ref[...], preferred_element_type=F32) + b_ref[...]
    z = ALPHA * x_ref[...] + (1.0 + g1_ref[0]) * y
    x1 = _layer_norm(z, lg_ref[...], lb_ref[...])
    x1_ref[...] = x1
    h = x1 * (1.0 + sc2_ref[0]) + sh2_ref[0]
    _rows_to_tiles(h_ref, h)
    lt = lax.dot_general(rwt_ref[...], h.astype(BF16), (((1,), (1,)), ((), ())),
                         preferred_element_type=F32) + rb_ref[...]
    e1, e2, w1, w2 = _route_rows(lt)
    eid_ref[0:1, :] = e1
    eid_ref[1:2, :] = e2
    gw_ref[0:1, :] = w1
    gw_ref[1:2, :] = w2


def _post_call(v, x, w, b, g1, lg, lb, sc2, sh2, rwt, rb, seq, tm=512):
    N, D = x.shape
    Dv = v.shape[1]
    E = rwt.shape[0]
    per_b = seq // tm
    row = lambda i: (i, 0)
    const = lambda i: (0, 0)
    bat = lambda i: (i // per_b, 0, 0)
    return pl.pallas_call(
        _post_kernel,
        grid=(N // tm,),
        in_specs=[
            pl.BlockSpec((tm, Dv), row),
            pl.BlockSpec((tm, D), row),
            pl.BlockSpec((Dv, D), const),
            pl.BlockSpec((1, D), const),
            pl.BlockSpec((1, 1, D), bat),
            pl.BlockSpec((1, D), const),
            pl.BlockSpec((1, D), const),
            pl.BlockSpec((1, 1, D), bat),
            pl.BlockSpec((1, 1, D), bat),
            pl.BlockSpec((E, D), const),
            pl.BlockSpec((E, 1), const),
        ],
        out_specs=[
            pl.BlockSpec((tm, D), row),
            pl.BlockSpec((tm * SUBLANES, LANES), row),
            pl.BlockSpec((2, tm), lambda i: (0, i)),
            pl.BlockSpec((2, tm), lambda i: (0, i)),
        ],
        out_shape=[
            jax.ShapeDtypeStruct((N, D), F32),
            jax.ShapeDtypeStruct((N * SUBLANES, LANES), F32),
            jax.ShapeDtypeStruct((2, N), I32),
            jax.ShapeDtypeStruct((2, N), F32),
        ],
        compiler_params=_cparams(("parallel",)),
        name="mixer_out_ln_route",
    )(v, x, w, b, g1, lg, lb, sc2, sh2, rwt, rb)


def _moe_kernel(bexp_ref, idx_hbm, h_hbm, wg_ref, wu_ref, wd_ref, ytok_hbm,
                idx_s, xbuf, ybuf, hm_ref, isem, gsem, ssem, *, nblk):
    del bexp_ref
    i = pl.program_id(0)
    slot = i % 2
    other = 1 - slot
    mb = hm_ref.shape[0]
    fd = hm_ref.shape[1]

    def tile(r):
        if isinstance(r, int):
            return pl.ds(r * SUBLANES, SUBLANES)
        return pl.ds(pl.multiple_of(r, SUBLANES), SUBLANES)

    def idx_copy(chunk):
        s4 = chunk % 4
        return pltpu.make_async_copy(idx_hbm.at[pl.ds(chunk * IDX_CHUNK, IDX_CHUNK)], idx_s.at[s4], isem.at[s4])

    def gather_start(chunk, xs, lo, hi):
        s4 = chunk % 4
        for j0 in range(lo, hi, DMA_BATCH):
            toks = [idx_s[s4, j] for j in range(j0, min(j0 + DMA_BATCH, hi))]
            for dj, t in enumerate(toks):
                pltpu.make_async_copy(h_hbm.at[tile(t)], xbuf.at[xs, tile(j0 + dj)], gsem.at[xs]).start(
                    priority=(j0 + dj) % 2)

    def scatter_start(chunk, ys, lo, hi):
        s4 = chunk % 4
        for j0 in range(lo, hi, DMA_BATCH):
            rows = [idx_s[s4, IDX_DST + j] for j in range(j0, min(j0 + DMA_BATCH, hi))]
            for dj, r in enumerate(rows):
                pltpu.make_async_copy(ybuf.at[ys, tile(j0 + dj)], ytok_hbm.at[tile(r)], ssem.at[ys]).start(
                    priority=(j0 + dj) % 2)

    def gather_wait(xs):
        pltpu.make_async_copy(h_hbm.at[pl.ds(0, mb * SUBLANES)], xbuf.at[xs], gsem.at[xs]).wait()

    def scatter_wait(ys):
        pltpu.make_async_copy(ybuf.at[ys], ytok_hbm.at[pl.ds(0, mb * SUBLANES)], ssem.at[ys]).wait()

    @pl.when(i == 0)
    def _():
        ybuf[1] = jnp.zeros(ybuf.shape[1:], F32)
        for c in range(3):
            idx_copy(c).start()
        for c in range(3):
            idx_copy(c).wait()
        gather_start(1, 0, 0, mb)

    @pl.when(i > 0)
    def _():
        idx_copy(i + 2).wait()

    @pl.when(i + 3 <= nblk + 1)
    def _():
        idx_copy(i + 3).start()

    gather_wait(slot)

    @pl.when(i > 0)
    def _():
        scatter_wait(slot)

    half = MOE_SEGMENTS // 2
    per_seg = mb // half
    nc = fd // half
    xc = _tiles_to_rows(xbuf.at[slot], mb).astype(BF16)
    dep = None
    for sg in range(MOE_SEGMENTS):
        if sg < half:
            xd = xc if dep is None else xc + dep
            if sg < half // 2:
                gather_start(i + 2, other, sg * 2 * per_seg, (sg + 1) * 2 * per_seg)
            else:
                scatter_start(i, other, (sg - half // 2) * 2 * per_seg, (sg - half // 2 + 1) * 2 * per_seg)
            z = _shifted_out_zero(xbuf[slot, 0:SUBLANES, :])
            ybuf[slot, 0:SUBLANES, :] = z
            z = z + _shifted_out_zero(ybuf[slot, pl.ds(pl.multiple_of(slot * SUBLANES, SUBLANES), SUBLANES), :])
            dep = jnp.concatenate([z[0:1, :]] * (xc.shape[1] // LANES), axis=1).astype(BF16)
            cs = slice(sg * nc, (sg + 1) * nc)
            g = jnp.dot(xd, wg_ref[0, :, cs], preferred_element_type=F32)
            u = jnp.dot(xd, wu_ref[0, :, cs], preferred_element_type=F32)
            hm_ref[:, cs] = (g * _sigmoid(g) * u).astype(BF16)
        else:
            n = sg - half
            dn = wd_ref.shape[2] // half
            y = jnp.dot(hm_ref[...], wd_ref[0, :, n * dn:(n + 1) * dn], preferred_element_type=F32)
            for s in range(dn // LANES):
                ybuf[slot, pl.ds(n * (dn // LANES) + s, mb, stride=SUBLANES), :] = y[:, s * LANES:(s + 1) * LANES]

    @pl.when(i == nblk - 1)
    def _():
        scatter_start(i + 1, slot, 0, mb)
        gather_wait(other)
        scatter_wait(other)
        scatter_wait(slot)


def _moe_call(bexp, idx, h3, wg, wu, wd, n_tok, layer):
    nblk = bexp.shape[0]
    _, D, Fd = wg.shape
    mb = MOE_ROWS
    wspec = lambda shape: pl.BlockSpec(shape, lambda i, be: (layer * N_EXPERTS + be[i], 0, 0))
    grid_spec = pltpu.PrefetchScalarGridSpec(
        num_scalar_prefetch=1,
        grid=(nblk,),
        in_specs=[
            pl.BlockSpec(memory_space=pl.ANY),
            pl.BlockSpec(memory_space=pl.ANY),
            wspec((1, D, Fd)),
            wspec((1, D, Fd)),
            wspec((1, Fd, D)),
        ],
        out_specs=pl.BlockSpec(memory_space=pl.ANY),
        scratch_shapes=[
            pltpu.SMEM((4, IDX_CHUNK), I32),
            pltpu.VMEM((2, mb * SUBLANES, LANES), F32),
            pltpu.VMEM((2, mb * SUBLANES, LANES), F32),
            pltpu.VMEM((mb, Fd), BF16),
            pltpu.SemaphoreType.DMA((4,)),
            pltpu.SemaphoreType.DMA((2,)),
            pltpu.SemaphoreType.DMA((2,)),
        ],
    )
    return pl.pallas_call(
        functools.partial(_moe_kernel, nblk=nblk),
        grid_spec=grid_spec,
        out_shape=jax.ShapeDtypeStruct(((2 * n_tok + 2 * mb) * SUBLANES, LANES), F32),
        compiler_params=_cparams(("arbitrary",)),
        name="moe_grouped_mlp",
    )(bexp, idx, h3, wg, wu, wd)


def _route_meta(eid, n_tok):
    mb = MOE_ROWS
    A = 2 * n_tok
    e = eid.reshape(-1)
    order = jnp.argsort(e).astype(I32)
    counts = jnp.bincount(e, length=N_EXPERTS).astype(I32)
    padded = (counts + mb - 1) // mb * mb
    pad_end = jnp.cumsum(padded)
    pad_start = pad_end - padded
    start = jnp.cumsum(counts) - counts
    nblk = A // mb + N_EXPERTS
    blk = jnp.arange(nblk, dtype=I32)
    bexp = jnp.minimum(jnp.searchsorted(pad_end, blk * mb, side="right"), N_EXPERTS - 1).astype(I32)
    j = jnp.arange(mb, dtype=I32)[None, :]
    r = blk[:, None] * mb + j - pad_start[bexp][:, None]
    valid = r < counts[bexp][:, None]
    a = order[jnp.clip(start[bexp][:, None] + r, 0, A - 1)]
    src = jnp.where(valid, jnp.where(a >= n_tok, a - n_tok, a), 0)
    dummy = A + (blk[:, None] % 2) * mb + j
    dst = jnp.where(valid, a, dummy)
    chunk = jnp.zeros((nblk + 2, IDX_CHUNK), I32)
    chunk = chunk.at[:, IDX_DST:IDX_DST + mb].set(jnp.broadcast_to(A + mb + j, (nblk + 2, mb)))
    chunk = chunk.at[1:nblk + 1, :mb].set(src)
    chunk = chunk.at[1:nblk + 1, IDX_DST:IDX_DST + mb].set(dst)
    return bexp, (chunk * SUBLANES).reshape(-1)


def _combine_kernel(x1_ref, y0_ref, y1_ref, gw_ref, g2_ref, lg_ref, lb_ref, o_ref):
    tm = x1_ref.shape[0]
    gw = gw_ref[...]
    eye = lax.broadcasted_iota(I32, (tm, tm), 0) == lax.broadcasted_iota(I32, (tm, tm), 1)
    w0 = jnp.sum(jnp.where(eye, jnp.broadcast_to(gw[0:1, :], (tm, tm)), 0.0), axis=1, keepdims=True)
    w1 = jnp.sum(jnp.where(eye, jnp.broadcast_to(gw[1:2, :], (tm, tm)), 0.0), axis=1, keepdims=True)
    y = w0 * _tiles_to_rows(y0_ref, tm) + w1 * _tiles_to_rows(y1_ref, tm)
    z = ALPHA * x1_ref[...] + (1.0 + g2_ref[0]) * y
    o_ref[...] = _layer_norm(z, lg_ref[...], lb_ref[...])


def _combine_call(x1, ytok, gw, g2, lg, lb, seq, tm=256):
    N, D = x1.shape
    per_b = seq // tm
    nt = N // tm
    const = lambda i: (0, 0)
    return pl.pallas_call(
        _combine_kernel,
        grid=(nt,),
        in_specs=[
            pl.BlockSpec((tm, D), lambda i: (i, 0)),
            pl.BlockSpec((tm * SUBLANES, LANES), lambda i: (i, 0)),
            pl.BlockSpec((tm * SUBLANES, LANES), lambda i: (nt + i, 0)),
            pl.BlockSpec((2, tm), lambda i: (0, i)),
            pl.BlockSpec((1, 1, D), lambda i: (i // per_b, 0, 0)),
            pl.BlockSpec((1, D), const),
            pl.BlockSpec((1, D), const),
        ],
        out_specs=pl.BlockSpec((tm, D), lambda i: (i, 0)),
        out_shape=jax.ShapeDtypeStruct((N, D), F32),
        compiler_params=_cparams(("parallel",)),
        name="moe_combine_ln",
    )(x1, ytok, ytok, gw, g2, lg, lb)


def _aug_constants():
    hw = N_HEADS * HEAD_W
    place_k = np.zeros((LANES, hw), np.float32)
    place_q = np.zeros((LANES, hw), np.float32)
    const_k = np.zeros((1, hw), np.float32)
    const_q = np.zeros((1, hw), np.float32)
    const_v = np.zeros((hw, 1), np.float32)
    for h in range(N_HEADS):
        for part in range(3):
            place_k[part * N_HEADS + h, h * HEAD_W + AUG + part] = -1.0
            place_q[part * N_HEADS + h, h * HEAD_W + AUG + 3 + part] = 1.0
            const_q[0, h * HEAD_W + AUG + part] = 1.0
            const_k[0, h * HEAD_W + AUG + 3 + part] = 1.0
        const_v[h * HEAD_W + (AUG if h % 2 == 0 else 0), 0] = 1.0
    return (jnp.asarray(place_k, BF16), jnp.asarray(place_q, BF16), jnp.asarray(const_k), jnp.asarray(const_q),
            jnp.asarray(const_v))


def _spread_heads(w, odd_high):
    D = w.shape[0]
    w4 = w.reshape(D, N_HEADS // 2, 2, HEAD_DIM)
    z = jnp.zeros_like(w4[:, :, 0])
    even = jnp.concatenate([w4[:, :, 0], z], axis=-1)
    odd = jnp.concatenate([z, w4[:, :, 1]], axis=-1) if odd_high else jnp.concatenate([w4[:, :, 1], z], axis=-1)
    return jnp.stack([even, odd], axis=2).reshape(D, N_HEADS * HEAD_W)


def _kvf_kernel(x_ref, sc_ref, sh_ref, wk_ref, wv_ref, wf_ref, bf_ref, pk_ref, ck_ref, cv_ref,
                k_ref, v_ref, g_ref, tri_ref, carry_ref, *, per_b):
    i = pl.program_id(0)
    tm = x_ref.shape[0]

    @pl.when(i == 0)
    def _():
        r = lax.broadcasted_iota(I32, (tm, tm), 0)
        c = lax.broadcasted_iota(I32, (tm, tm), 1)
        tri_ref[...] = jnp.where(r >= c, 1.0, 0.0).astype(BF16)

    @pl.when(i % per_b == 0)
    def _():
        carry_ref[...] = jnp.zeros(carry_ref.shape, F32)

    h = (x_ref[...] * (1.0 + sc_ref[0]) + sh_ref[0]).astype(BF16)
    z = jnp.dot(h, wf_ref[...], preferred_element_type=F32) + bf_ref[...]
    lf = jnp.minimum(z, 0.0) - jnp.log(1.0 + jnp.exp(-jnp.abs(z)))
    p1 = lf.astype(BF16)
    r1 = lf - p1.astype(F32)
    p2 = r1.astype(BF16)
    p3 = (r1 - p2.astype(F32)).astype(BF16)
    tri = tri_ref[...]
    cs = (jnp.dot(tri, p1, preferred_element_type=F32) + jnp.dot(tri, p2, preferred_element_type=F32)
          + jnp.dot(tri, p3, preferred_element_type=F32)) + carry_ref[...]
    carry_ref[...] = cs[tm - 1:tm, :]
    lane = lax.broadcasted_iota(I32, cs.shape, 1)
    gl = jnp.where(lane < N_HEADS, cs * LOG2E, 0.0)
    g_hi = gl.astype(BF16).astype(F32)
    g_r = gl - g_hi
    g_mid = g_r.astype(BF16).astype(F32)
    g_lo = (g_r - g_mid).astype(BF16).astype(F32)
    gcat = (g_hi + pltpu.roll(g_mid, N_HEADS, 1) + pltpu.roll(g_lo, 2 * N_HEADS, 1)).astype(BF16)
    g_ref[...] = gcat
    k_ref[...] = (jnp.dot(h, wk_ref[...], preferred_element_type=F32)
                  + jnp.dot(gcat, pk_ref[...], preferred_element_type=F32) + ck_ref[...]).astype(BF16)
    vt = lax.dot_general(wv_ref[...], h, (((1,), (1,)), ((), ())), preferred_element_type=F32)
    v_ref[0] = (vt + cv_ref[...]).astype(BF16)


def _kvf_call(x, sc, sh, wk, wv, wf, bf, pk, ck, cv, seq, tm=512):
    N, D = x.shape
    HW = wk.shape[1]
    per_b = seq // tm
    const = lambda i: (0, 0)
    bat = lambda i: (i // per_b, 0, 0)
    row = lambda i: (i, 0)
    return pl.pallas_call(
        functools.partial(_kvf_kernel, per_b=per_b),
        grid=(N // tm,),
        in_specs=[
            pl.BlockSpec((tm, D), row),
            pl.BlockSpec((1, 1, D), bat),
            pl.BlockSpec((1, 1, D), bat),
            pl.BlockSpec((D, HW), const),
            pl.BlockSpec((HW, D), const),
            pl.BlockSpec((D, LANES), const),
            pl.BlockSpec((1, LANES), const),
            pl.BlockSpec((LANES, HW), const),
            pl.BlockSpec((1, HW), const),
            pl.BlockSpec((HW, 1), const),
        ],
        out_specs=[
            pl.BlockSpec((tm, HW), row),
            pl.BlockSpec((1, HW, tm), lambda i: (i // per_b, 0, i % per_b)),
            pl.BlockSpec((tm, LANES), row),
        ],
        out_shape=[
            jax.ShapeDtypeStruct((N, HW), BF16),
            jax.ShapeDtypeStruct((N // seq, HW, seq), BF16),
            jax.ShapeDtypeStruct((N, LANES), BF16),
        ],
        scratch_shapes=[pltpu.VMEM((tm, tm), BF16), pltpu.VMEM((1, LANES), F32)],
        compiler_params=_cparams(("arbitrary",)),
        name="shared_kv_forget",
    )(x, sc, sh, wk, wv, wf, bf, pk, ck, cv)


def _qproj_kernel(x_ref, sc_ref, sh_ref, w_ref, g_ref, pq_ref, cq_ref, o_ref, *, qscale):
    h = (x_ref[...] * (1.0 + sc_ref[0]) + sh_ref[0]).astype(BF16)
    q = jnp.dot(h, w_ref[...], preferred_element_type=F32) * qscale
    o_ref[...] = (q + jnp.dot(g_ref[...], pq_ref[...], preferred_element_type=F32) + cq_ref[...]).astype(BF16)


def _qproj_call(x, sc, sh, w, gcat, pq, cq, seq, tm=512):
    N, D = x.shape
    HW = w.shape[1]
    per_b = seq // tm
    const = lambda i: (0, 0)
    bat = lambda i: (i // per_b, 0, 0)
    row = lambda i: (i, 0)
    return pl.pallas_call(
        functools.partial(_qproj_kernel, qscale=HEAD_DIM ** -0.5 * LOG2E),
        grid=(N // tm,),
        in_specs=[
            pl.BlockSpec((tm, D), row),
            pl.BlockSpec((1, 1, D), bat),
            pl.BlockSpec((1, 1, D), bat),
            pl.BlockSpec((D, HW), const),
            pl.BlockSpec((tm, LANES), row),
            pl.BlockSpec((LANES, HW), const),
            pl.BlockSpec((1, HW), const),
        ],
        out_specs=pl.BlockSpec((tm, HW), row),
        out_shape=jax.ShapeDtypeStruct((N, HW), BF16),
        compiler_params=_cparams(("parallel",)),
        name="q_proj",
    )(x, sc, sh, w, gcat, pq, cq)


def _attn_kernel(q_ref, k_ref, vt_ref, o_ref, m_ref, acc_ref, sa_ref, sb_ref, *, tk, nh):
    qi = pl.program_id(2)
    tq = q_ref.shape[0]
    m_ref[...] = jnp.full(m_ref.shape, NEG_BIG, F32)
    acc_ref[...] = jnp.zeros(acc_ref.shape, F32)

    def scores(j, c):
        k0 = pl.multiple_of(j * tk, tk)
        hs = slice(c * HEAD_W, (c + 1) * HEAD_W)
        return lax.dot_general(k_ref[pl.ds(k0, tk), hs], q_ref[:, hs], (((1,), (1,)), ((), ())),
                               preferred_element_type=F32)

    def update(j, c, s, masked):
        k0 = pl.multiple_of(j * tk, tk)
        if masked:
            key = k0 + lax.broadcasted_iota(I32, (tk, tq), 0)
            qry = qi * tq + lax.broadcasted_iota(I32, (tk, tq), 1)
            s = jnp.where(qry >= key, s, NEG_BIG)
        m_old = m_ref[c]
        m_new = jnp.maximum(m_old, jnp.max(s, axis=0, keepdims=True))
        p = jnp.exp2(s - m_new)
        a = jnp.exp2(m_old - m_new)
        pv = jnp.dot(vt_ref[0, c * HEAD_W:(c + 1) * HEAD_W, pl.ds(k0, tk)], p.astype(BF16),
                     preferred_element_type=F32)
        acc_ref[c] = a * acc_ref[c] + pv
        m_ref[c] = m_new

    def stage(j, src, dst):
        for c in range(nh):
            dst[c] = scores(j + 1, c)
        for c in range(nh):
            update(j, c, src[c], False)

    for c in range(nh):
        sa_ref[c] = scores(0, c)

    def body(jj, carry):
        stage(2 * jj, sa_ref, sb_ref)
        stage(2 * jj + 1, sb_ref, sa_ref)
        return carry

    lax.fori_loop(0, qi // 2, body, 0)

    @pl.when(qi % 2 == 1)
    def _():
        stage(qi - 1, sa_ref, sb_ref)
        for c in range(nh):
            update(qi, c, sb_ref[c], True)

    @pl.when(qi % 2 == 0)
    def _():
        for c in range(nh):
            update(qi, c, sa_ref[c], True)

    outs = []
    for c in range(nh):
        a = acc_ref[c]
        if c % 2 == 0:
            outs.append(a[0:HEAD_DIM, :] / a[AUG:AUG + 1, :])
        else:
            outs.append(a[HEAD_DIM:, :] / a[0:1, :])
    o_ref[...] = jnp.concatenate(outs, axis=0).T.astype(o_ref.dtype)


def _attn_call(q, k, vt, batch, seq, tq=ATTN_TILE, nh=ATTN_HEADS):
    tk = tq
    N, HW = q.shape
    wn = nh * HEAD_W
    ng = HW // wn
    nq = seq // tq
    return pl.pallas_call(
        functools.partial(_attn_kernel, tk=tk, nh=nh),
        grid=(batch, ng, nq),
        in_specs=[
            pl.BlockSpec((tq, wn), lambda b, h, i: (b * nq + i, h)),
            pl.BlockSpec((seq, wn), lambda b, h, i: (b, h)),
            pl.BlockSpec((1, wn, seq), lambda b, h, i: (b, h, 0)),
        ],
        out_specs=pl.BlockSpec((tq, nh * HEAD_DIM), lambda b, h, i: (b * nq + i, h)),
        out_shape=jax.ShapeDtypeStruct((N, ng * nh * HEAD_DIM), BF16),
        scratch_shapes=[
            pltpu.VMEM((nh, 1, tq), F32),
            pltpu.VMEM((nh, HEAD_W, tq), F32),
            pltpu.VMEM((nh, tk, tq), F32),
            pltpu.VMEM((nh, tk, tq), F32),
        ],
        compiler_params=_cparams(("parallel", "parallel", "arbitrary")),
        name="fox_attention",
    )(q, k, vt)


def kernel(x, c, w_ada, b_ada, ln_mix_g, ln_mix_b, ln_ffn_g, ln_ffn_b, w_pw1, b_pw1, w_dw, b_dw, conv_ln_g,
           conv_ln_b, w_pw2, b_pw2, w_ada_kv, b_ada_kv, w_kvf, b_f, w_q, w_o, router_w, router_b, w_gate, w_up,
           w_down):
    B, S, D = x.shape
    N = B * S
    xf = x.reshape(N, D)

    mod = _mod_call(c, w_ada, b_ada[:, None, :])
    mod_kv = _mod_call(c, w_ada_kv[None], b_ada_kv[None, None, :])

    def mvec(l, idx):
        return mod[l, :, idx * D:(idx + 1) * D][:, None, :]

    rwt = router_w.T.astype(BF16)
    rb = router_b[:, None].astype(F32)
    zero_b = jnp.zeros((1, D), F32)
    place_k, place_q, const_k, const_q, const_v = _aug_constants()
    wg_all = w_gate.astype(BF16).reshape((DEPTH * N_EXPERTS,) + w_gate.shape[2:])
    wu_all = w_up.astype(BF16).reshape((DEPTH * N_EXPERTS,) + w_up.shape[2:])
    wd_all = w_down.astype(BF16).reshape((DEPTH * N_EXPERTS,) + w_down.shape[2:])
    k = v = gcat = None

    for l in range(DEPTH):
        sh1, sc1, g1, sh2, sc2, g2 = (mvec(l, t) for t in range(6))
        if l < N_A_LAYERS:
            u = _glu_call(xf, sc1, sh1, w_pw1[l].astype(BF16), b_pw1[l][None, :], S)
            mix = _conv_call(u, w_dw[l][:, 0, :], b_dw[l][None, :], conv_ln_g[l][None, :], conv_ln_b[l][None, :],
                             B, S)
            w_out, b_out = w_pw2[l].astype(BF16), b_pw2[l][None, :]
        else:
            jl = l - N_A_LAYERS
            if l == N_A_LAYERS:
                wf = jnp.zeros((D, LANES), F32).at[:, :N_HEADS].set(w_kvf[:, 2 * D:]).astype(BF16)
                bf = jnp.zeros((1, LANES), F32).at[0, :N_HEADS].set(b_f)
                k, v, gcat = _kvf_call(xf, mod_kv[0, :, D:][:, None, :], mod_kv[0, :, :D][:, None, :],
                                       _spread_heads(w_kvf[:, :D], False).astype(BF16),
                                       _spread_heads(w_kvf[:, D:2 * D], True).T.astype(BF16),
                                       wf, bf, place_k, const_k, const_v, S)
            q = _qproj_call(xf, sc1, sh1, _spread_heads(w_q[jl], False).astype(BF16), gcat, place_q, const_q, S)
            mix = _attn_call(q, k, v, B, S)
            w_out, b_out = w_o[jl].astype(BF16), zero_b
        x1, h3, eid, gw = _post_call(mix, xf, w_out, b_out, g1, ln_mix_g[l][None, :], ln_mix_b[l][None, :],
                                     sc2, sh2, rwt, rb, S)
        bexp, idx = _route_meta(eid, N)
        ytok = _moe_call(bexp, idx, h3, wg_all, wu_all, wd_all, N, l)
        xf = _combine_call(x1, ytok, gw, g2, ln_ffn_g[l][None, :], ln_ffn_b[l][None, :], S)
    return xf.reshape(B, S, D)
```
